```python
import math, functools
import jax, jax.numpy as jnp
from jax import lax
import numpy as np

D_MODEL = 1024
BATCH = 2
SEQ = 8192
DEPTH = 2
DEC_BATCH = 32
DEC_SEQ = 1
PAST_LEN = 8192
PAGE_SIZE = 128

A_HEADS = 4
A_QK_DIM = D_MODEL // 16
A_V_DIM = 2 * A_QK_DIM
A_WIDTH = A_HEADS * A_V_DIM
Q_BLOCK = 128
B_WIDTH = D_MODEL // 2
B_BLOCKS = 8
B_BLOCK_W = B_WIDTH // B_BLOCKS
CONV_W = 4
LRU_C = 8.0
C_HEADS = 4
C_HEAD_DIM = D_MODEL // 8
C_WIDTH = C_HEADS * C_HEAD_DIM
CHUNK = 128
N_BRANCH = 3
BRANCH_W = D_MODEL // 2
EPS = 1e-6
IN_SPLITS = (A_WIDTH, A_WIDTH, A_WIDTH, A_WIDTH, B_WIDTH, B_WIDTH,
             C_WIDTH, C_WIDTH, C_WIDTH, C_WIDTH, C_WIDTH, C_HEADS, C_HEADS, N_BRANCH * D_MODEL)
N_IN = sum(IN_SPLITS)

kernel_name = 'hybrid_gated_diffattn_rglru_mlstm_step'

F32 = jnp.float32


def _rmsnorm(x, g):
    xf = x.astype(F32)
    return xf * lax.rsqrt(jnp.mean(xf * xf, axis=-1, keepdims=True) + EPS) * g.astype(F32)


def _alibi_slopes():
    return jnp.asarray(2.0 ** (-8.0 * np.arange(1, A_HEADS + 1) / A_HEADS), dtype=F32)


def _diff_lambda(lam_p, lam_init):
    lam_p = lam_p.astype(F32)
    return jnp.exp(jnp.sum(lam_p[0] * lam_p[1])) - jnp.exp(jnp.sum(lam_p[2] * lam_p[3])) + lam_init


def _diff_attend(q, qpos, k, v, kpos, lam):
    s = jnp.einsum('bqhcd,bkhcd->bhcqk', q.astype(F32), k.astype(F32)) * (A_QK_DIM ** -0.5)
    dist = (qpos[:, None] - kpos[None, :]).astype(F32)
    s = s - _alibi_slopes()[:, None, None, None] * dist
    s = jnp.where(kpos[None, :] <= qpos[:, None], s, -jnp.inf)
    p = jax.nn.softmax(s, axis=-1)
    w = p[:, :, 0] - lam * p[:, :, 1]
    return jnp.einsum('bhqk,bkhd->bqhd', w, v.astype(F32))


def _attn_prompt(q, k, v, lam):
    B, T = q.shape[:2]
    nb = T // Q_BLOCK
    kpos = jnp.arange(T)
    qb = q.reshape(B, nb, Q_BLOCK, A_HEADS, 2, A_QK_DIM).swapaxes(0, 1)

    def block(args):
        qi, i = args
        qpos = i * Q_BLOCK + jnp.arange(Q_BLOCK)
        return _diff_attend(qi, qpos, k, v, kpos, lam)

    o = lax.map(block, (qb, jnp.arange(nb)))
    return o.swapaxes(0, 1).reshape(B, T, A_HEADS, A_V_DIM)


def _attn_sample(past_k, past_v, q, k, v, lam):
    P = past_k.shape[1]
    T = q.shape[1]
    k_all = jnp.concatenate([past_k.astype(F32), k.astype(F32)], axis=1)
    v_all = jnp.concatenate([past_v.astype(F32), v.astype(F32)], axis=1)
    qpos = P + jnp.arange(T)
    kpos = jnp.arange(P + T)
    return _diff_attend(q, qpos, k_all, v_all, kpos, lam)


def _linear_recurrence(a, u, h0):
    u = u.at[:, 0].add(a[:, 0] * h0)

    def comb(l, r):
        return (l[0] * r[0], r[0] * l[1] + r[1])

    _, h = lax.associative_scan(comb, (a, u), axis=1)
    return h


def _rglru_branch(xb, z, conv_buf, h0, p):
    B, T, _ = xb.shape
    xcat = jnp.concatenate([conv_buf.astype(F32), xb], axis=1)
    cw = p['b_conv_w'].astype(F32)
    y = p['b_conv_b'].astype(F32) + sum(cw[j] * xcat[:, j:j + T] for j in range(CONV_W))
    conv_new = xcat[:, T:]
    yb = y.reshape(B, T, B_BLOCKS, B_BLOCK_W)
    r = jax.nn.sigmoid(jnp.einsum('btnc,ncd->btnd', yb, p['b_wa'].astype(F32)).reshape(B, T, B_WIDTH)
                       + p['b_ba'].astype(F32))
    i = jax.nn.sigmoid(jnp.einsum('btnc,ncd->btnd', yb, p['b_wx'].astype(F32)).reshape(B, T, B_WIDTH)
                       + p['b_bx'].astype(F32))
    log_a = -LRU_C * r * jax.nn.softplus(-p['b_lambda'].astype(F32))
    a = jnp.exp(log_a)
    u = jnp.sqrt(-jnp.expm1(2.0 * log_a)) * (i * y)
    h = _linear_recurrence(a, u, h0.astype(F32))
    return h * jax.nn.silu(z), h[:, -1], conv_new


def _mlstm_chunk(state, xs):
    C, n, m = state
    q, k, v, ig, lf = xs
    L = q.shape[1]
    b = jnp.cumsum(lf, axis=1)
    g = b + m[:, None, :]
    causal = jnp.arange(L)[:, None] >= jnp.arange(L)[None, :]
    dlog = b[:, :, None, :] - b[:, None, :, :] + ig[:, None, :, :]
    dlog = jnp.where(causal[None, :, :, None], dlog, -jnp.inf)
    m_t = jnp.maximum(g, jnp.max(dlog, axis=2))
    w = jnp.exp(dlog - m_t[:, :, None, :]) * jnp.einsum('bthd,bshd->btsh', q, k)
    inter = jnp.exp(g - m_t)
    num = jnp.einsum('btsh,bshd->bthd', w, v) + inter[..., None] * jnp.einsum('bthk,bhkv->bthv', q, C)
    den = jnp.sum(w, axis=2) + inter * jnp.einsum('bthk,bhk->bth', q, n)
    h = num / jnp.maximum(jnp.abs(den), jnp.exp(-m_t))[..., None]
    m_new = m_t[:, -1]
    wt = jnp.exp(b[:, -1:] - b + ig - m_new[:, None])
    decay = jnp.exp(b[:, -1] + m - m_new)
    C_new = decay[..., None, None] * C + jnp.einsum('bsh,bshk,bshv->bhkv', wt, k, v)
    n_new = decay[..., None] * n + jnp.einsum('bsh,bshk->bhk', wt, k)
    return (C_new, n_new, m_new), h


def _mlstm_branch(cq, ck, cv, co, cz, ci, cf, state, p, chunked):
    B, T = cq.shape[:2]
    shp = (B, T, C_HEADS, C_HEAD_DIM)
    q = cq.reshape(shp)
    k = ck.reshape(shp) * (C_HEAD_DIM ** -0.5)
    v = cv.reshape(shp)
    gb = p['c_gate_b'].astype(F32)
    ig = ci + gb[:C_HEADS]
    lf = jax.nn.log_sigmoid(cf + gb[C_HEADS:])
    if chunked:
        nc = T // CHUNK
        xs = tuple(t.reshape((B, nc, CHUNK) + t.shape[2:]).swapaxes(0, 1) for t in (q, k, v, ig, lf))
        new_state, hs = lax.scan(_mlstm_chunk, state, xs)
        h = hs.swapaxes(0, 1).reshape(shp)
    else:
        new_state, h = _mlstm_chunk(state, (q, k, v, ig, lf))
    h = _rmsnorm(jax.nn.sigmoid(co).reshape(shp) * h, p['c_outnorm_g']).reshape(B, T, C_WIDTH)
    return h * jax.nn.silu(cz), new_state


def _merge(x, a_out, b_out, c_out, mg, p):
    B, T = x.shape[:2]
    br = jnp.stack([a_out, b_out, c_out], axis=2)
    proj = jnp.einsum('btcw,cwd->btcd', br, p['w_branch'].astype(F32))
    gates = jax.nn.sigmoid(mg.reshape(B, T, N_BRANCH, D_MODEL)
                           + p['merge_b'].astype(F32).reshape(N_BRANCH, D_MODEL))
    merged = jnp.sum(gates * proj, axis=2)
    return x + jnp.einsum('btd,de->bte', merged, p['w_out'].astype(F32)).astype(x.dtype)


def _layer(x, p, lam_init, attn_fn, conv_buf, lru_h0, mlstm_state, chunked):
    B, T = x.shape[:2]
    h = _rmsnorm(x, p['norm_g'])
    u = jnp.einsum('btd,dn->btn', h, p['w_in'].astype(F32))
    aq, ak, av, az, bx, bz, cq, ck, cv, co, cz, ci, cf, mg = jnp.split(
        u, np.cumsum(IN_SPLITS)[:-1].tolist(), axis=-1)
    q = _rmsnorm(aq.reshape(B, T, A_HEADS, 2, A_QK_DIM), p['a_qnorm_g'])
    k = _rmsnorm(ak.reshape(B, T, A_HEADS, 2, A_QK_DIM), p['a_knorm_g'])
    v = av.reshape(B, T, A_HEADS, A_V_DIM)
    lam = _diff_lambda(p['a_lambda'], lam_init)
    o = attn_fn(q, k, v, lam)
    a_out = (_rmsnorm(o, p['a_subln_g']) * (1.0 - lam_init)).reshape(B, T, A_WIDTH) * jax.nn.silu(az)
    b_out, lru_h, conv_new = _rglru_branch(bx, bz, conv_buf, lru_h0, p)
    c_out, mstate = _mlstm_branch(cq, ck, cv, co, cz, ci, cf, mlstm_state, p, chunked)
    y = _merge(x, a_out, b_out, c_out, mg, p)
    return y, k.reshape(B, T, A_HEADS, 2 * A_QK_DIM), v, conv_new, lru_h, mstate


def setup_inputs(seed: int = 0) -> dict:
    key = jax.random.key(seed)
    ks = jax.random.split(key, 32)
    n_pages = PAST_LEN // PAGE_SIZE
    n_used = DEC_BATCH * n_pages
    n_phys = (5 * n_used + 3) // 4

    def nrm(k, shape, s=1.0):
        return s * jax.random.normal(k, shape, F32)

    page_table = jax.random.permutation(ks[0], n_phys)[:n_used].reshape(DEC_BATCH, n_pages).astype(jnp.int32)
    ua = jax.random.uniform(ks[22], (DEPTH, B_WIDTH), F32, 0.9, 0.999)
    a_base = ua ** (1.0 / LRU_C)
    b_lambda = jnp.log(a_base) - jnp.log1p(-a_base)
    f_bias = jnp.broadcast_to(jnp.linspace(3.0, 6.0, C_HEADS, dtype=F32), (DEPTH, C_HEADS))
    c_gate_b = jnp.concatenate([nrm(ks[23], (DEPTH, C_HEADS), 0.1),
                                f_bias + nrm(ks[24], (DEPTH, C_HEADS), 0.1)], axis=-1)
    return {
        'x_prompt': nrm(ks[1], (BATCH, SEQ, D_MODEL)),
        'x_sample': nrm(ks[2], (DEC_BATCH, DEC_SEQ, D_MODEL)),
        'cache_k': nrm(ks[3], (DEPTH, n_phys, PAGE_SIZE, A_HEADS, 2 * A_QK_DIM)),
        'cache_v': nrm(ks[4], (DEPTH, n_phys, PAGE_SIZE, A_HEADS, A_V_DIM)),
        'page_table': page_table,
        'state_conv': nrm(ks[5], (DEPTH, DEC_BATCH, CONV_W - 1, B_WIDTH)),
        'state_lru_h': nrm(ks[6], (DEPTH, DEC_BATCH, B_WIDTH), 0.5),
        'state_mlstm_C': nrm(ks[7], (DEPTH, DEC_BATCH, C_HEADS, C_HEAD_DIM, C_HEAD_DIM), 0.5),
        'state_mlstm_n': nrm(ks[8], (DEPTH, DEC_BATCH, C_HEADS, C_HEAD_DIM), 0.5),
        'state_mlstm_m': nrm(ks[9], (DEPTH, DEC_BATCH, C_HEADS)),
        'norm_g': 1.0 + nrm(ks[10], (DEPTH, D_MODEL), 0.02),
        'w_in': nrm(ks[11], (DEPTH, D_MODEL, N_IN), D_MODEL ** -0.5),
        'a_qnorm_g': 1.0 + nrm(ks[12], (DEPTH, A_QK_DIM), 0.02),
        'a_knorm_g': 1.0 + nrm(ks[13], (DEPTH, A_QK_DIM), 0.02),
        'a_lambda': nrm(ks[14], (DEPTH, 4, A_QK_DIM), 0.1),
        'a_subln_g': 1.0 + nrm(ks[15], (DEPTH, A_V_DIM), 0.02),
        'b_conv_w': nrm(ks[16], (DEPTH, CONV_W, B_WIDTH), CONV_W ** -0.5),
        'b_conv_b': nrm(ks[17], (DEPTH, B_WIDTH), 0.01),
        'b_wa': nrm(ks[18], (DEPTH, B_BLOCKS, B_BLOCK_W, B_BLOCK_W), B_BLOCK_W ** -0.5),
        'b_ba': nrm(ks[19], (DEPTH, B_WIDTH), 0.01),
        'b_wx': nrm(ks[20], (DEPTH, B_BLOCKS, B_BLOCK_W, B_BLOCK_W), B_BLOCK_W ** -0.5),
        'b_bx': nrm(ks[21], (DEPTH, B_WIDTH), 0.01),
        'b_lambda': b_lambda,
        'c_gate_b': c_gate_b,
        'c_outnorm_g': 1.0 + nrm(ks[25], (DEPTH, C_HEAD_DIM), 0.02),
        'merge_b': nrm(ks[26], (DEPTH, N_BRANCH * D_MODEL), 0.01),
        'w_branch': nrm(ks[27], (DEPTH, N_BRANCH, BRANCH_W, D_MODEL), BRANCH_W ** -0.5),
        'w_out': nrm(ks[28], (DEPTH, D_MODEL, D_MODEL), D_MODEL ** -0.5),
    }


def reference(x_prompt, x_sample, cache_k, cache_v, page_table, state_conv, state_lru_h,
              state_mlstm_C, state_mlstm_n, state_mlstm_m, norm_g, w_in, a_qnorm_g, a_knorm_g,
              a_lambda, a_subln_g, b_conv_w, b_conv_b, b_wa, b_ba, b_wx, b_bx, b_lambda,
              c_gate_b, c_outnorm_g, merge_b, w_branch, w_out):
    B = x_prompt.shape[0]
    DB = x_sample.shape[0]
    past = page_table.shape[1] * cache_k.shape[2]
    zero_conv = jnp.zeros((B, CONV_W - 1, B_WIDTH), F32)
    zero_h = jnp.zeros((B, B_WIDTH), F32)
    zero_m = (jnp.zeros((B, C_HEADS, C_HEAD_DIM, C_HEAD_DIM), F32),
              jnp.zeros((B, C_HEADS, C_HEAD_DIM), F32),
              jnp.zeros((B, C_HEADS), F32))
    kp, vp, ksm, vsm, cp, csm, hp, hsm, Cp, Csm, n_p, nsm, mp, msm = ([] for _ in range(14))
    xp, xs = x_prompt, x_sample
    for l in range(DEPTH):
        lam_init = 0.8 - 0.6 * math.exp(-0.3 * l)
        p = dict(norm_g=norm_g[l], w_in=w_in[l], a_qnorm_g=a_qnorm_g[l], a_knorm_g=a_knorm_g[l],
                 a_lambda=a_lambda[l], a_subln_g=a_subln_g[l], b_conv_w=b_conv_w[l],
                 b_conv_b=b_conv_b[l], b_wa=b_wa[l], b_ba=b_ba[l], b_wx=b_wx[l], b_bx=b_bx[l],
                 b_lambda=b_lambda[l], c_gate_b=c_gate_b[l], c_outnorm_g=c_outnorm_g[l],
                 merge_b=merge_b[l], w_branch=w_branch[l], w_out=w_out[l])
        xp, k_, v_, c_, h_, (C_, n_, m_) = _layer(xp, p, lam_init, _attn_prompt,
                                                  zero_conv, zero_h, zero_m, True)
        kp.append(k_); vp.append(v_); cp.append(c_); hp.append(h_)
        Cp.append(C_); n_p.append(n_); mp.append(m_)
        past_k = cache_k[l][page_table].reshape(DB, past, A_HEADS, 2, A_QK_DIM)
        past_v = cache_v[l][page_table].reshape(DB, past, A_HEADS, A_V_DIM)
        attn_s = functools.partial(_attn_sample, past_k, past_v)
        sstate = (state_mlstm_C[l].astype(F32), state_mlstm_n[l].astype(F32), state_mlstm_m[l].astype(F32))
        xs, k_, v_, c_, h_, (C_, n_, m_) = _layer(xs, p, lam_init, attn_s,
                                                  state_conv[l], state_lru_h[l], sstate, False)
        ksm.append(k_); vsm.append(v_); csm.append(c_); hsm.append(h_)
        Csm.append(C_); nsm.append(n_); msm.append(m_)
    return (xp, xs, jnp.stack(kp), jnp.stack(vp), jnp.stack(ksm), jnp.stack(vsm),
            jnp.stack(cp), jnp.stack(csm), jnp.stack(hp), jnp.stack(hsm),
            jnp.stack(Cp), jnp.stack(Csm), jnp.stack(n_p), jnp.stack(nsm),
            jnp.stack(mp), jnp.stack(msm))
```

```python
import functools
import math

import jax
import jax.numpy as jnp
from jax import lax
from jax.experimental import pallas as pl
from jax.experimental.pallas import tpu as pltpu

F32 = jnp.float32
BF16 = jnp.bfloat16

EPS = 1e-6
LRU_C = 8.0
LOG2E = 1.4426950408889634
NEG_INF = float("-inf")

LANES = 128
SUBLANES = 8
BF16_ROWS = 16
MXU_DIM = 256
VMEM_LIMIT = 48 * 1024 * 1024


def _cparams(*sem):
    return pltpu.CompilerParams(dimension_semantics=sem, vmem_limit_bytes=VMEM_LIMIT)


def _sigmoid(x):
    return 1.0 / (1.0 + jnp.exp(-x))


def _silu(x):
    return x * _sigmoid(x)


def _softplus(x):
    return jnp.maximum(x, 0.0) + jnp.log1p(jnp.exp(-jnp.abs(x)))


def _log_sigmoid(x):
    return -_softplus(-x)


def _expm1(x):
    u = jnp.exp(x)
    um1 = u - 1.0
    return jnp.where(um1 == 0.0, x, jnp.where(um1 == -1.0, -1.0, um1 * x / jnp.log(u)))


def _split3(x):
    hi = x.astype(BF16)
    r1 = x - hi.astype(F32)
    mid = r1.astype(BF16)
    lo = (r1 - mid.astype(F32)).astype(BF16)
    return hi, mid, lo


def _group_meansq(x, bd, group):
    x2 = x * x
    hi = x2.astype(BF16)
    lo = (x2 - hi.astype(F32)).astype(BF16)
    cols = []
    for c in range(x.shape[1] // MXU_DIM):
        sl = slice(c * MXU_DIM, (c + 1) * MXU_DIM)
        cols.append(jnp.dot(hi[:, sl], bd, preferred_element_type=F32)
                    + jnp.dot(lo[:, sl], bd, preferred_element_type=F32))
    return jnp.concatenate(cols, axis=1) * (1.0 / group)


def _block_diag_ones(group):
    idx = jnp.arange(MXU_DIM) // group
    return (idx[:, None] == idx[None, :]).astype(BF16)


def _inproj_kernel(x_ref, g_ref, w_ref, o_ref, h_scr):
    @pl.when(pl.program_id(1) == 0)
    def _():
        x = x_ref[...]
        ms = jnp.mean(x * x, axis=-1, keepdims=True)
        h_scr[...] = (x * lax.rsqrt(ms + EPS) * g_ref[...]).astype(BF16)

    o_ref[...] = jnp.dot(h_scr[...], w_ref[...], preferred_element_type=F32)


def _inproj(x, g, w, tn):
    m, d = x.shape
    n = w.shape[1]
    tm = min(m, 1024)
    return pl.pallas_call(
        _inproj_kernel,
        grid=(m // tm, n // tn),
        in_specs=[pl.BlockSpec((tm, d), lambda i, j: (i, 0)),
                  pl.BlockSpec((1, d), lambda i, j: (0, 0)),
                  pl.BlockSpec((d, tn), lambda i, j: (0, j))],
        out_specs=pl.BlockSpec((tm, tn), lambda i, j: (i, j)),
        out_shape=jax.ShapeDtypeStruct((m, n), F32),
        scratch_shapes=[pltpu.VMEM((tm, d), BF16)],
        compiler_params=_cparams("parallel", "arbitrary"),
        name="inproj",
    )(x, g, w)


def _qk_norm(a_ref, gq_ref, gk_ref, bd_ref, qk_dim):
    width = a_ref.shape[1] // 3
    bd = bd_ref[...]
    aq = a_ref[:, 0:width]
    ak = a_ref[:, width:2 * width]
    qn = aq * lax.rsqrt(_group_meansq(aq, bd, qk_dim) + EPS) * gq_ref[...]
    kn = ak * lax.rsqrt(_group_meansq(ak, bd, qk_dim) + EPS) * gk_ref[...]
    return qn, kn, a_ref[:, 2 * width:3 * width]


def _qkv_prompt_kernel(a_ref, gq_ref, gk_ref, bd_ref,
                       q_ref, k_ref, v_ref, kout_ref, vout_ref, *, heads, qk_dim, v_dim):
    qn, kn, v = _qk_norm(a_ref, gq_ref, gk_ref, bd_ref, qk_dim)
    kout_ref[...] = kn
    vout_ref[...] = v
    qs = (qn * (qk_dim ** -0.5 * LOG2E)).astype(BF16)
    kb = kn.astype(BF16)
    vb = v.astype(BF16)
    rows = v.shape[0]
    lane = lax.broadcasted_iota(jnp.int32, (rows, MXU_DIM - v_dim), 1)
    ones_col = jnp.where(lane == 0, 1.0, 0.0).astype(BF16)
    for h in range(heads):
        q_ref[0, h] = qs[:, h * 2 * qk_dim:(h + 1) * 2 * qk_dim]
        k_ref[0, h] = kb[:, h * 2 * qk_dim:(h + 1) * 2 * qk_dim]
        v_ref[0, h] = jnp.concatenate([vb[:, h * v_dim:(h + 1) * v_dim], ones_col], axis=1)


def _qkv_prompt(u, gq, gk, bd, batch, seq, heads, qk_dim, v_dim, a_col):
    m = batch * seq
    width = heads * v_dim
    tm = min(seq, 512)
    nt = seq // tm
    kern = functools.partial(_qkv_prompt_kernel, heads=heads, qk_dim=qk_dim, v_dim=v_dim)
    row = lambda c: pl.BlockSpec((tm, width), lambda b, i, c=c: (b * nt + i, c))
    const = lambda shape: pl.BlockSpec(shape, lambda b, i: (0,) * len(shape))
    return pl.pallas_call(
        kern,
        grid=(batch, nt),
        in_specs=[pl.BlockSpec((tm, 3 * width), lambda b, i: (b * nt + i, a_col)),
                  const((1, width)), const((1, width)), const((MXU_DIM, MXU_DIM))],
        out_specs=[pl.BlockSpec((1, heads, tm, 2 * qk_dim), lambda b, i: (b, 0, i, 0)),
                   pl.BlockSpec((1, heads, tm, 2 * qk_dim), lambda b, i: (b, 0, i, 0)),
                   pl.BlockSpec((1, heads, tm, MXU_DIM), lambda b, i: (b, 0, i, 0)),
                   row(0), row(0)],
        out_shape=[jax.ShapeDtypeStruct((batch, heads, seq, 2 * qk_dim), BF16),
                   jax.ShapeDtypeStruct((batch, heads, seq, 2 * qk_dim), BF16),
                   jax.ShapeDtypeStruct((batch, heads, seq, MXU_DIM), BF16),
                   jax.ShapeDtypeStruct((m, width), F32),
                   jax.ShapeDtypeStruct((m, width), F32)],
        compiler_params=_cparams("parallel", "parallel"),
        name="qkv_prompt",
    )(u, gq, gk, bd)


def _qkv_sample_kernel(a_ref, gq_ref, gk_ref, bd_ref, q_ref, kout_ref, vout_ref, *, qk_dim):
    qn, kn, v = _qk_norm(a_ref, gq_ref, gk_ref, bd_ref, qk_dim)
    q_ref[...] = qn * (qk_dim ** -0.5 * LOG2E)
    kout_ref[...] = kn
    vout_ref[...] = v


def _qkv_sample(u, gq, gk, bd, qk_dim, width, a_col):
    m = u.shape[0]
    kern = functools.partial(_qkv_sample_kernel, qk_dim=qk_dim)
    row = lambda c: pl.BlockSpec((m, width), lambda i, c=c: (0, c))
    const = lambda shape: pl.BlockSpec(shape, lambda i: (0,) * len(shape))
    return pl.pallas_call(
        kern,
        grid=(1,),
        in_specs=[pl.BlockSpec((m, 3 * width), lambda i: (0, a_col)),
                  const((1, width)), const((1, width)), const((MXU_DIM, MXU_DIM))],
        out_specs=[row(0), row(0), row(0)],
        out_shape=[jax.ShapeDtypeStruct((m, width), F32)] * 3,
        compiler_params=_cparams("arbitrary"),
        name="qkv_sample",
    )(u, gq, gk, bd)


def _diff_lambda(lam_ref, lam_init):
    lp = lam_ref[...]
    s1 = jnp.sum(lp[0:1] * lp[1:2], axis=1, keepdims=True)
    s2 = jnp.sum(lp[2:3] * lp[3:4], axis=1, keepdims=True)
    return jnp.exp(s1) - jnp.exp(s2) + lam_init


def _attn_epilogue(o1, l1, o2, l2, lam, g, az, lam_init):
    o = o1 / l1 - lam * (o2 / l2)
    ms = jnp.mean(o * o, axis=-1, keepdims=True)
    return o * lax.rsqrt(ms + EPS) * g * (1.0 - lam_init) * _silu(az)


def _attn_prompt_kernel(slope_ref, lam_ref, q_ref, k_ref, v_ref, az_ref, g_ref,
                        boff_ref, bdiag_ref, o_ref, qs_scr, m_scr, acc_scr,
                        *, tq, tk, qk_dim, v_dim, lam_init):
    h = pl.program_id(1)
    i = pl.program_id(2)
    ratio = tk // tq
    beta = slope_ref[h] * LOG2E

    q = q_ref[0, 0]
    lane = lax.broadcasted_iota(jnp.int32, q.shape, 1)
    zero = jnp.zeros_like(q)
    qs_scr[0:tq, :] = jnp.where(lane < qk_dim, q, zero)
    qs_scr[tq:2 * tq, :] = jnp.where(lane >= qk_dim, q, zero)
    m_scr[...] = jnp.full(m_scr.shape, NEG_INF, F32)
    acc_scr[...] = jnp.zeros(acc_scr.shape, F32)

    def tile(kt, bias, shift):
        start = pl.multiple_of(kt * tk, tk)
        k = k_ref[0, 0, pl.ds(start, tk), :]
        v = v_ref[0, 0, pl.ds(start, tk), :]
        s = lax.dot_general(qs_scr[...], k, (((1,), (1,)), ((), ())),
                            preferred_element_type=F32)
        s = s + jnp.concatenate([bias, bias], axis=0)
        r = jnp.max(s, axis=1, keepdims=True)
        m_old = m_scr[...]
        m_new = jnp.maximum(m_old, r + shift)
        e = jnp.exp2(s - (m_new - shift))
        alpha = jnp.exp2(m_old - m_new)
        acc_scr[...] = alpha * acc_scr[...] + jnp.dot(e.astype(BF16), v,
                                                      preferred_element_type=F32)
        m_scr[...] = m_new

    n_full = (i * tq) // tk

    def body(kt, carry):
        shift = beta * (kt * tk - i * tq).astype(F32)
        tile(kt, boff_ref[0], shift)
        return carry

    lax.fori_loop(0, n_full, body, 0)
    tile(n_full, bdiag_ref[0, i % ratio], jnp.float32(0.0))

    lam = _diff_lambda(lam_ref, lam_init)
    acc = acc_scr[...]
    o_ref[...] = _attn_epilogue(
        acc[0:tq, 0:v_dim], acc[0:tq, v_dim:v_dim + 1],
        acc[tq:2 * tq, 0:v_dim], acc[tq:2 * tq, v_dim:v_dim + 1],
        lam, g_ref[...], az_ref[...], lam_init).astype(o_ref.dtype)


def _alibi_slopes(heads):
    return jnp.asarray([2.0 ** (-8.0 * (j + 1) / heads) for j in range(heads)], F32)


def _attn_prompt(q, k, v, u, lam_p, g, az_col, lam_init, tq, tk):
    batch, heads, seq, qk2 = q.shape
    qk_dim = qk2 // 2
    v_dim = g.shape[1]
    tq = min(tq, seq)
    tk = min(tk, seq)
    ratio = tk // tq
    nq = seq // tq
    slopes = _alibi_slopes(heads)
    beta = (slopes * LOG2E)[:, None, None]
    rel = (jnp.arange(tk)[None, :] - jnp.arange(tq)[:, None]).astype(F32)
    boff = beta * rel[None]
    offs = (jnp.arange(ratio) * tq).astype(F32)[:, None, None]
    reld = rel[None] - offs
    bdiag = jnp.where(reld[None] <= 0, beta[:, None] * reld[None], NEG_INF)
    kern = functools.partial(_attn_prompt_kernel, tq=tq, tk=tk, qk_dim=qk_dim, v_dim=v_dim,
                             lam_init=lam_init)
    return pl.pallas_call(
        kern,
        grid=(batch, heads, nq),
        in_specs=[pl.BlockSpec(memory_space=pltpu.SMEM),
                  pl.BlockSpec(lam_p.shape, lambda b, h, i: (0, 0)),
                  pl.BlockSpec((1, 1, tq, qk2), lambda b, h, i: (b, h, i, 0)),
                  pl.BlockSpec((1, 1, seq, qk2), lambda b, h, i: (b, h, 0, 0)),
                  pl.BlockSpec((1, 1, seq, MXU_DIM), lambda b, h, i: (b, h, 0, 0)),
                  pl.BlockSpec((tq, v_dim), lambda b, h, i: (b * nq + i, az_col + h)),
                  pl.BlockSpec((1, v_dim), lambda b, h, i: (0, 0)),
                  pl.BlockSpec((1, tq, tk), lambda b, h, i: (h, 0, 0)),
                  pl.BlockSpec((1, ratio, tq, tk), lambda b, h, i: (h, 0, 0, 0))],
        out_specs=pl.BlockSpec((tq, v_dim), lambda b, h, i: (b * nq + i, h)),
        out_shape=jax.ShapeDtypeStruct((batch * seq, heads * v_dim), BF16),
        scratch_shapes=[pltpu.VMEM((2 * tq, qk2), BF16),
                        pltpu.VMEM((2 * tq, 1), F32),
                        pltpu.VMEM((2 * tq, MXU_DIM), F32)],
        compiler_params=_cparams("parallel", "parallel", "arbitrary"),
        name="attn_prompt",
    )(slopes, lam_p, q, k, v, u, g, boff, bdiag)


def _attn_sample_kernel(pt_ref, lam_ref, q_ref, kx_ref, vx_ref, az_ref, g_ref, beta_ref,
                        bias_ref, *rest, pg, rows, past, heads, v_dim, lam_init):
    k_refs = rest[:pg]
    v_refs = rest[pg:2 * pg]
    o_ref = rest[2 * pg]
    m_scr, l_scr, acc_scr = rest[2 * pg + 1:]
    g_idx = pl.program_id(1)
    n_groups = pl.num_programs(1)
    tokens = rows // heads

    @pl.when(g_idx == 0)
    def _():
        m_scr[...] = jnp.full(m_scr.shape, NEG_INF, F32)
        l_scr[...] = jnp.zeros(l_scr.shape, F32)
        acc_scr[...] = jnp.zeros(acc_scr.shape, F32)

    q = q_ref[0]
    beta = beta_ref[...]

    def update(s, vals):
        r = jnp.max(s, axis=1, keepdims=True)
        m_old = m_scr[...]
        m_new = jnp.maximum(m_old, r)
        e = jnp.exp2(s - m_new)
        alpha = jnp.exp2(m_old - m_new)
        pv = vals(e.astype(BF16))
        l_scr[...] = alpha * l_scr[...] + jnp.sum(e, axis=1, keepdims=True)
        acc_scr[...] = alpha * acc_scr[...] + pv
        m_scr[...] = m_new

    s_parts = []
    for j in range(pg):
        kp = k_refs[j][0, 0].astype(BF16)
        s_parts.append(lax.dot_general(q, kp, (((1,), (1,)), ((), ())),
                                       preferred_element_type=F32))
    s = jnp.concatenate(s_parts, axis=1)
    dist0 = (past - g_idx * (pg * tokens)).astype(F32)
    s = s + bias_ref[...] - beta * dist0

    def past_vals(e):
        out = jnp.zeros((e.shape[0], v_dim), F32)
        for j in range(pg):
            out = out + jnp.dot(e[:, j * rows:(j + 1) * rows], v_refs[j][0, 0].astype(BF16),
                                preferred_element_type=F32)
        return out

    update(s, past_vals)

    @pl.when(g_idx == n_groups - 1)
    def _():
        kx = kx_ref[0].astype(BF16).astype(F32)
        vx = vx_ref[0].astype(BF16).astype(F32)
        sx = jnp.sum(q.astype(F32) * kx, axis=1, keepdims=True)
        m_old = m_scr[...]
        m_new = jnp.maximum(m_old, sx)
        ex = jnp.exp2(sx - m_new)
        alpha = jnp.exp2(m_old - m_new)
        l_scr[...] = alpha * l_scr[...] + ex
        acc_scr[...] = alpha * acc_scr[...] + ex * vx
        m_scr[...] = m_new

        lam = _diff_lambda(lam_ref, lam_init)
        acc = acc_scr[...]
        l = l_scr[...]
        az = az_ref[0]
        outs = []
        for h in range(heads):
            outs.append(_attn_epilogue(
                acc[2 * h:2 * h + 1], l[2 * h:2 * h + 1],
                acc[2 * h + 1:2 * h + 2], l[2 * h + 1:2 * h + 2],
                lam, g_ref[...], az[:, h * v_dim:(h + 1) * v_dim], lam_init))
        o_ref[0] = jnp.concatenate(outs, axis=1).astype(o_ref.dtype)


def _attn_sample(layer, page_table, cache_k, cache_v, qn, k_new, v_new, az, lam_p, g, lam_init):
    db, n_pages = page_table.shape
    rows, v_dim = cache_k.shape[2], cache_k.shape[3]
    heads = qn.shape[1] // v_dim
    qk_dim = v_dim // 2
    tokens = rows // heads
    past = n_pages * tokens
    pg = math.gcd(n_pages, 8)
    n_groups = n_pages // pg
    nrow = 2 * heads

    slopes = _alibi_slopes(heads)
    beta_rows = jnp.repeat(slopes * LOG2E, 2)[:, None]
    q3 = qn.reshape(db, heads, 1, v_dim)
    comp = (jnp.arange(v_dim)[None, :] // qk_dim) == jnp.arange(2)[:, None]
    qmat = jnp.where(comp[None, None], q3, 0.0).reshape(db, nrow, v_dim).astype(BF16)
    kx = jnp.repeat(k_new.reshape(db, heads, v_dim), 2, axis=1)
    vx = jnp.repeat(v_new.reshape(db, heads, v_dim), 2, axis=1)
    lane = jnp.arange(pg * rows)
    row_head = jnp.arange(nrow) // 2
    visible = (lane[None, :] % heads) == row_head[:, None]
    bias = jnp.where(visible, beta_rows * (lane[None, :] // heads).astype(F32), NEG_INF)

    kern = functools.partial(_attn_sample_kernel, pg=pg, rows=rows, past=past, heads=heads,
                             v_dim=v_dim, lam_init=lam_init)
    width = heads * v_dim

    def page_spec(j):
        return pl.BlockSpec((1, 1, rows, v_dim),
                            lambda b, gi, pt, j=j: (layer, pt[b, gi * pg + j], 0, 0))

    const = lambda shape: pl.BlockSpec(shape, lambda b, gi, pt: (0,) * len(shape))
    per_seq = lambda shape: pl.BlockSpec((1,) + shape, lambda b, gi, pt: (b, 0, 0))
    grid_spec = pltpu.PrefetchScalarGridSpec(
        num_scalar_prefetch=1,
        grid=(db, n_groups),
        in_specs=[const(lam_p.shape), per_seq((nrow, v_dim)), per_seq((nrow, v_dim)),
                  per_seq((nrow, v_dim)), per_seq((1, width)), const((1, v_dim)),
                  const((nrow, 1)), const((nrow, pg * rows))]
                 + [page_spec(j) for j in range(pg)] * 2,
        out_specs=per_seq((1, width)),
        scratch_shapes=[pltpu.VMEM((nrow, 1), F32), pltpu.VMEM((nrow, 1), F32),
                        pltpu.VMEM((nrow, v_dim), F32)],
    )
    out = pl.pallas_call(
        kern,
        grid_spec=grid_spec,
        out_shape=jax.ShapeDtypeStruct((db, 1, width), BF16),
        compiler_params=_cparams("parallel", "arbitrary"),
        name="attn_sample",
    )(page_table, lam_p, qmat, kx, vx, az.reshape(db, 1, width), g, beta_rows, bias,
      *([cache_k] * pg), *([cache_v] * pg))
    return out.reshape(db, width)


def _lru_gates(y, wa_ref, ba_ref, wx_ref, bx_ref, lam_ref):
    yb = y.astype(BF16)
    r = _sigmoid(jnp.dot(yb, wa_ref[...], preferred_element_type=F32) + ba_ref[...])
    ig = _sigmoid(jnp.dot(yb, wx_ref[...], preferred_element_type=F32) + bx_ref[...])
    log_a = -LRU_C * r * _softplus(-lam_ref[...])
    a = jnp.exp(log_a)
    u = jnp.sqrt(-_expm1(2.0 * log_a)) * (ig * y)
    return a, u


def _lru_prompt_kernel(xz_ref, cw_ref, cb_ref, wa_ref, ba_ref, wx_ref, bx_ref, lam_ref,
                       o_ref, hlast_ref, tail_ref, tail_scr, h_scr, a_scr, u_scr, hs_scr,
                       *, tt, conv_w):
    t = pl.program_id(1)

    @pl.when(t == 0)
    def _():
        tail_scr[...] = jnp.zeros(tail_scr.shape, F32)
        h_scr[...] = jnp.zeros(h_scr.shape, F32)

    width = xz_ref.shape[1] // 2
    x = xz_ref[:, 0:width]
    xe = jnp.concatenate([tail_scr[...], x], axis=0)
    cw = cw_ref[...]
    y = cb_ref[...] + cw[conv_w - 1:conv_w] * x
    for j in range(1, conv_w):
        y = y + cw[conv_w - 1 - j:conv_w - j] * pltpu.roll(xe, j, 0)[SUBLANES:]
    tail_scr[...] = x[tt - SUBLANES:]
    tail_ref[0] = x[tt - SUBLANES:]

    a, u = _lru_gates(y, wa_ref, ba_ref, wx_ref, bx_ref, lam_ref)
    a_scr[...] = a
    u_scr[...] = u

    def group(gi, h):
        base = pl.multiple_of(gi * SUBLANES, SUBLANES)
        for r in range(SUBLANES):
            h = a_scr[pl.ds(base + r, 1), :] * h + u_scr[pl.ds(base + r, 1), :]
            hs_scr[pl.ds(base + r, 1), :] = h
        return h

    h = lax.fori_loop(0, tt // SUBLANES, group, h_scr[...])
    h_scr[...] = h
    hlast_ref[0] = h
    o_ref[...] = (hs_scr[...] * _silu(xz_ref[:, width:2 * width])).astype(o_ref.dtype)


def _lru_prompt(u, p, batch, seq, xz_col):
    width = p["cw"].shape[1]
    conv_w = p["cw"].shape[0]
    tt = min(seq, 512)
    nt = seq // tt
    kern = functools.partial(_lru_prompt_kernel, tt=tt, conv_w=conv_w)
    row = lambda c: pl.BlockSpec((tt, width), lambda b, t, c=c: (b * nt + t, c))
    const = lambda a: pl.BlockSpec(a.shape, lambda b, t: (0,) * a.ndim)
    ws = [p["cw"], p["cb"], p["wa"], p["ba"], p["wx"], p["bx"], p["lam"]]
    return pl.pallas_call(
        kern,
        grid=(batch, nt),
        in_specs=[pl.BlockSpec((tt, 2 * width), lambda b, t: (b * nt + t, xz_col))]
                 + [const(a) for a in ws],
        out_specs=[row(0),
                   pl.BlockSpec((1, 1, width), lambda b, t: (b, 0, 0)),
                   pl.BlockSpec((1, SUBLANES, width), lambda b, t: (b, 0, 0))],
        out_shape=[jax.ShapeDtypeStruct((batch * seq, width), BF16),
                   jax.ShapeDtypeStruct((batch, 1, width), F32),
                   jax.ShapeDtypeStruct((batch, SUBLANES, width), F32)],
        scratch_shapes=[pltpu.VMEM((SUBLANES, width), F32), pltpu.VMEM((1, width), F32),
                        pltpu.VMEM((tt, width), F32), pltpu.VMEM((tt, width), F32),
                        pltpu.VMEM((tt, width), F32)],
        compiler_params=_cparams("parallel", "arbitrary"),
        name="lru_prompt",
    )(u, *ws)


def _head_norm_gate(hcat, o_gate, z_gate, g_tiled, heads, hd):
    x = _sigmoid(o_gate) * hcat
    parts = []
    for h in range(heads):
        xh = x[:, h * hd:(h + 1) * hd]
        ms = jnp.mean(xh * xh, axis=-1, keepdims=True)
        parts.append(xh * lax.rsqrt(ms + EPS))
    return jnp.concatenate(parts, axis=1) * g_tiled * _silu(z_gate)


def _mlstm_prompt_kernel(cg_ref, gr_ref, gbr_ref, gbc_ref,
                         gn_ref, out_ref, c_ref, n_ref, m_ref, *, heads, hd, chunk):
    width = heads * hd
    c_idx = pl.program_id(1)

    @pl.when(c_idx == 0)
    def _():
        c_ref[...] = jnp.zeros(c_ref.shape, F32)
        n_ref[...] = jnp.zeros(n_ref.shape, F32)
        m_ref[...] = jnp.zeros(m_ref.shape, F32)

    rows = lax.broadcasted_iota(jnp.int32, (chunk, chunk), 0)
    cols = lax.broadcasted_iota(jnp.int32, (chunk, chunk), 1)
    causal = rows >= cols
    tri_lower = jnp.where(causal, 1.0, 0.0).astype(BF16)
    tri_upper = jnp.where(rows <= cols, 1.0, 0.0).astype(BF16)

    gcol = cg_ref[:, 5 * width:5 * width + LANES] + gbr_ref[...]
    grow = gr_ref[...] + gbc_ref[...]
    lf_col = _log_sigmoid(gcol)
    lf_row = _log_sigmoid(grow)
    b_col = sum(jnp.dot(tri_lower, part, preferred_element_type=F32) for part in _split3(lf_col))
    b_row = sum(jnp.dot(part, tri_upper, preferred_element_type=F32) for part in _split3(lf_row))

    q_all = cg_ref[:, 0:width]
    k_all = cg_ref[:, width:2 * width] * (hd ** -0.5)
    v_all = cg_ref[:, 2 * width:3 * width]
    m_all = m_ref[0]
    n_all = n_ref[0]
    h_parts = []
    m_parts = []
    for h in range(heads):
        sl = slice(h * hd, (h + 1) * hd)
        bc = b_col[:, heads + h:heads + h + 1]
        igc = gcol[:, h:h + 1]
        br = b_row[heads + h:heads + h + 1, :]
        igr = grow[h:h + 1, :]
        m_prev = m_all[:, h:h + 1]
        dlog = jnp.where(causal, bc - br + igr, NEG_INF)
        g = bc + m_prev
        m_t = jnp.maximum(g, jnp.max(dlog, axis=1, keepdims=True))
        qh = q_all[:, sl].astype(BF16)
        kf = k_all[:, sl]
        vh = v_all[:, sl].astype(BF16)
        qk = lax.dot_general(qh, kf.astype(BF16), (((1,), (1,)), ((), ())),
                             preferred_element_type=F32)
        w = jnp.exp(dlog - m_t) * qk
        inter = jnp.exp(g - m_t)
        c_old = c_ref[0, h]
        n_old = n_all[h:h + 1, :]
        num = (jnp.dot(w.astype(BF16), vh, preferred_element_type=F32)
               + inter * jnp.dot(qh, c_old.astype(BF16), preferred_element_type=F32))
        den = (jnp.sum(w, axis=1, keepdims=True)
               + inter * jnp.sum(q_all[:, sl] * n_old, axis=1, keepdims=True))
        h_parts.append(num / jnp.maximum(jnp.abs(den), jnp.exp(-m_t)))
        m_new = m_t[chunk - 1:chunk, :]
        b_last = bc[chunk - 1:chunk, :]
        wt = jnp.exp(b_last - bc + igc - m_new)
        decay = jnp.exp(b_last + m_prev - m_new)
        kw = kf * wt
        c_ref[0, h] = decay * c_old + jnp.dot(kw.T.astype(BF16), vh, preferred_element_type=F32)
        n_ref[0, h:h + 1, :] = decay * n_old + jnp.sum(kw, axis=0, keepdims=True)
        m_parts.append(m_new)
    m_ref[0] = jnp.concatenate(m_parts + [jnp.zeros((1, LANES - heads), F32)], axis=1)
    out_ref[...] = _head_norm_gate(jnp.concatenate(h_parts, axis=1), cg_ref[:, 3 * width:4 * width],
                                   cg_ref[:, 4 * width:5 * width], gn_ref[...],
                                   heads, hd).astype(out_ref.dtype)


def _mlstm_prompt(u, ugate_t, gb_row, gb_col, gn, batch, seq, heads, hd, chunk, cg_col):
    width = heads * hd
    nc = seq // chunk
    kern = functools.partial(_mlstm_prompt_kernel, heads=heads, hd=hd, chunk=chunk)
    row = lambda c: pl.BlockSpec((chunk, width), lambda b, t, c=c: (b * nc + t, c))
    const = lambda a: pl.BlockSpec(a.shape, lambda b, t: (0,) * a.ndim)
    return pl.pallas_call(
        kern,
        grid=(batch, nc),
        in_specs=[pl.BlockSpec((chunk, 6 * width), lambda b, t: (b * nc + t, cg_col)),
                  pl.BlockSpec((BF16_ROWS, chunk), lambda b, t: (0, b * nc + t)),
                  const(gb_row), const(gb_col), const(gn)],
        out_specs=[row(0),
                   pl.BlockSpec((1, heads, hd, hd), lambda b, t: (b, 0, 0, 0)),
                   pl.BlockSpec((1, heads, hd), lambda b, t: (b, 0, 0)),
                   pl.BlockSpec((1, 1, LANES), lambda b, t: (b, 0, 0))],
        out_shape=[jax.ShapeDtypeStruct((batch * seq, width), BF16),
                   jax.ShapeDtypeStruct((batch, heads, hd, hd), F32),
                   jax.ShapeDtypeStruct((batch, heads, hd), F32),
                   jax.ShapeDtypeStruct((batch, 1, LANES), F32)],
        compiler_params=_cparams("parallel", "arbitrary"),
        name="mlstm_prompt",
    )(u, ugate_t, gb_row, gb_col, gn)


def _lru_sample_kernel(xz_ref, conv_ref, h0_ref, cw_ref, cb_ref, wa_ref, ba_ref, wx_ref,
                       bx_ref, lam_ref, o_ref, h_ref, convn_ref, *, conv_w):
    width = xz_ref.shape[1] // 2
    x = xz_ref[:, 0:width]
    cw = cw_ref[...]
    y = cb_ref[...] + cw[conv_w - 1:conv_w] * x
    for j in range(conv_w - 1):
        y = y + cw[j:j + 1] * conv_ref[j]
    for j in range(conv_w - 2):
        convn_ref[j] = conv_ref[j + 1]
    convn_ref[conv_w - 2] = x
    a, u = _lru_gates(y, wa_ref, ba_ref, wx_ref, bx_ref, lam_ref)
    h = a * h0_ref[...] + u
    h_ref[...] = h
    o_ref[...] = (h * _silu(xz_ref[:, width:2 * width])).astype(o_ref.dtype)


def _lru_sample(u, conv_t, h0, p, xz_col):
    m = u.shape[0]
    width = p["cw"].shape[1]
    conv_w = p["cw"].shape[0]
    kern = functools.partial(_lru_sample_kernel, conv_w=conv_w)
    row = lambda c: pl.BlockSpec((m, width), lambda i, c=c: (0, c))
    full = lambda a: pl.BlockSpec(a.shape, lambda i: (0,) * a.ndim)
    ws = [p["cw"], p["cb"], p["wa"], p["ba"], p["wx"], p["bx"], p["lam"]]
    return pl.pallas_call(
        kern,
        grid=(1,),
        in_specs=[pl.BlockSpec((m, 2 * width), lambda i: (0, xz_col)), full(conv_t), full(h0)]
                 + [full(a) for a in ws],
        out_specs=[row(0), row(0), full(conv_t)],
        out_shape=[jax.ShapeDtypeStruct((m, width), BF16),
                   jax.ShapeDtypeStruct((m, width), F32),
                   jax.ShapeDtypeStruct(conv_t.shape, F32)],
        compiler_params=_cparams("arbitrary"),
        name="lru_sample",
    )(u, conv_t, h0, *ws)


def _mlstm_sample_kernel(cg_ref, gbr_ref, gn_ref,
                         c_ref, n_ref, m_ref, out_ref, cn_ref, nn_ref, mn_ref, h_scr,
                         *, heads, hd, sb):
    width = heads * hd
    gcol = cg_ref[:, 5 * width:5 * width + LANES] + gbr_ref[...]
    ig = gcol[:, 0:heads]
    lf = _log_sigmoid(gcol[:, heads:2 * heads])
    m_prev = m_ref[...]
    g = lf + m_prev
    m_t = jnp.maximum(g, ig)
    w_in = jnp.exp(ig - m_t)
    inter = jnp.exp(g - m_t)
    floor = jnp.exp(-m_t)
    mn_ref[...] = m_t
    q_all = cg_ref[:, 0:width]
    k_all = cg_ref[:, width:2 * width] * (hd ** -0.5)
    v_all = cg_ref[:, 2 * width:3 * width]
    for h in range(heads):
        sl = slice(h * hd, (h + 1) * hd)
        qk = jnp.sum(q_all[:, sl] * k_all[:, sl], axis=1, keepdims=True)
        for s in range(sb):
            q_row = q_all[s:s + 1, sl]
            k_row = k_all[s:s + 1, sl]
            v_row = v_all[s:s + 1, sl]
            q_col = jnp.broadcast_to(q_row, (hd, hd)).T
            k_col = jnp.broadcast_to(k_row, (hd, hd)).T
            c_old = c_ref[s, h]
            n_old = n_ref[s, h:h + 1, :]
            w_s = w_in[s:s + 1, h:h + 1]
            i_s = inter[s:s + 1, h:h + 1]
            num = w_s * qk[s:s + 1] * v_row + i_s * jnp.sum(q_col * c_old, axis=0, keepdims=True)
            den = w_s * qk[s:s + 1] + i_s * jnp.sum(q_row * n_old, axis=1, keepdims=True)
            h_scr[s:s + 1, sl] = num / jnp.maximum(jnp.abs(den), floor[s:s + 1, h:h + 1])
            cn_ref[s, h] = i_s * c_old + (w_s * k_col) * v_row
            nn_ref[s, h:h + 1, :] = i_s * n_old + w_s * k_row
    out_ref[...] = _head_norm_gate(h_scr[...], cg_ref[:, 3 * width:4 * width],
                                   cg_ref[:, 4 * width:5 * width], gn_ref[...],
                                   heads, hd).astype(out_ref.dtype)


def _mlstm_sample(u, gb_row, gn, c0, n0, m0, heads, hd, cg_col):
    m = u.shape[0]
    width = heads * hd
    sb = SUBLANES
    kern = functools.partial(_mlstm_sample_kernel, heads=heads, hd=hd, sb=sb)
    row = lambda c: pl.BlockSpec((sb, width), lambda i, c=c: (i, c))
    const = lambda a: pl.BlockSpec(a.shape, lambda i: (0,) * a.ndim)
    c_spec = pl.BlockSpec((sb, heads, hd, hd), lambda i: (i, 0, 0, 0))
    n_spec = pl.BlockSpec((sb, heads, hd), lambda i: (i, 0, 0))
    m_spec = pl.BlockSpec((sb, heads), lambda i: (i, 0))
    return pl.pallas_call(
        kern,
        grid=(m // sb,),
        in_specs=[pl.BlockSpec((sb, 6 * width), lambda i: (i, cg_col)), const(gb_row), const(gn),
                  c_spec, n_spec, m_spec],
        out_specs=[row(0), c_spec, n_spec, m_spec],
        out_shape=[jax.ShapeDtypeStruct((m, width), BF16),
                   jax.ShapeDtypeStruct(c0.shape, F32),
                   jax.ShapeDtypeStruct(n0.shape, F32),
                   jax.ShapeDtypeStruct(m0.shape, F32)],
        scratch_shapes=[pltpu.VMEM((sb, width), F32)],
        compiler_params=_cparams("parallel"),
        name="mlstm_sample",
    )(u, gb_row, gn, c0, n0, m0)


def _merge_kernel(x_ref, a_ref, b_ref, c_ref, mg_ref, mb_ref, wb_ref, wo_ref, y_ref, *, d):
    merged = None
    for j, br in enumerate((a_ref, b_ref, c_ref)):
        proj = jnp.dot(br[...], wb_ref[j], preferred_element_type=F32)
        gate = _sigmoid(mg_ref[:, j * d:(j + 1) * d] + mb_ref[:, j * d:(j + 1) * d])
        merged = gate * proj if merged is None else merged + gate * proj
    y_ref[...] = x_ref[...] + jnp.dot(merged.astype(BF16), wo_ref[...],
                                      preferred_element_type=F32)


def _merge(x, a, b, c, u, mg_col, mb, wb, wo):
    m, d = x.shape
    width = a.shape[1]
    tm = min(m, 256)
    kern = functools.partial(_merge_kernel, d=d)
    row = lambda w: pl.BlockSpec((tm, w), lambda i: (i, 0))
    const = lambda arr: pl.BlockSpec(arr.shape, lambda i: (0,) * arr.ndim)
    return pl.pallas_call(
        kern,
        grid=(m // tm,),
        in_specs=[row(d), row(width), row(width), row(width),
                  pl.BlockSpec((tm, mb.shape[1]), lambda i: (i, mg_col)),
                  const(mb), const(wb), const(wo)],
        out_specs=row(d),
        out_shape=jax.ShapeDtypeStruct((m, d), F32),
        compiler_params=_cparams("parallel"),
        name="merge",
    )(x, a, b, c, u, mb, wb, wo)


def _expand_block_diag(w):
    nb, c, _ = w.shape
    eye = jnp.eye(nb, dtype=w.dtype)
    return (eye[:, None, :, None] * w[:, :, None, :]).reshape(nb * c, nb * c)


def kernel(x_prompt, x_sample, cache_k, cache_v, page_table, state_conv, state_lru_h, state_mlstm_C, state_mlstm_n, state_mlstm_m, norm_g, w_in, a_qnorm_g, a_knorm_g, a_lambda, a_subln_g, b_conv_w, b_conv_b, b_wa, b_ba, b_wx, b_bx, b_lambda, c_gate_b, c_outnorm_g, merge_b, w_branch, w_out):
    batch, seq, d = x_prompt.shape
    db = x_sample.shape[0]
    depth = w_in.shape[0]
    heads = cache_k.shape[3]
    v_dim = cache_v.shape[4]
    qk_dim = a_qnorm_g.shape[1]
    a_width = heads * v_dim
    b_width = b_conv_w.shape[2]
    c_heads = state_mlstm_C.shape[2]
    c_hd = state_mlstm_C.shape[3]
    c_width = c_heads * c_hd
    n_branch = w_branch.shape[1]
    chunk = min(seq, 128)
    assert a_width == b_width == c_width, "column-block addressing assumes equal branch widths"
    width = a_width
    n_gate = 2 * c_heads
    n_merge = n_branch * d
    off_c, off_g, off_b, off_az, off_a, off_mg = 0, 5 * width, 6 * width, 8 * width, 9 * width, 12 * width
    n_packed = off_mg + n_merge
    assert off_mg % n_merge == 0 and width % LANES == 0 and n_gate <= BF16_ROWS
    cg_col, xz_col, az_col, a_col, mg_col = 0, off_b // (2 * width), off_az // v_dim, off_a // (3 * width), off_mg // n_merge
    tn = n_packed // 6
    assert n_packed % 6 == 0 and tn % LANES == 0
    src_a, src_b, src_c = 0, 4 * width, 6 * width
    src_g = src_c + 5 * width

    n_phys, page = cache_k.shape[1], cache_k.shape[2]
    ck2 = cache_k.reshape(depth, n_phys, page * heads, 2 * qk_dim)
    cv2 = cache_v.reshape(depth, n_phys, page * heads, v_dim)
    bd = _block_diag_ones(qk_dim)

    xp = x_prompt.reshape(batch * seq, d)
    xs = x_sample.reshape(db, d)
    outs = {name: [] for name in ("kp", "vp", "ks", "vs", "cp", "cs", "hp", "hs",
                                  "Cp", "Cs", "np", "ns", "mp", "ms")}
    conv_w = b_conv_w.shape[1]

    for l in range(depth):
        lam_init = 0.8 - 0.6 * math.exp(-0.3 * l)
        wl = w_in[l]
        w_packed = jnp.concatenate([
            wl[:, src_c:src_c + 5 * width],
            jnp.pad(wl[:, src_g:src_g + n_gate], ((0, 0), (0, width - n_gate))),
            wl[:, src_b:src_b + 2 * width],
            wl[:, src_a + 3 * width:src_a + 4 * width],
            wl[:, src_a:src_a + 3 * width],
            wl[:, src_g + n_gate:]], axis=1).astype(BF16)
        g_norm = norm_g[l][None, :]
        gq = jnp.tile(a_qnorm_g[l], 2 * heads)[None, :]
        gk = jnp.tile(a_knorm_g[l], 2 * heads)[None, :]
        g_sub = a_subln_g[l][None, :]
        lru = dict(cw=b_conv_w[l], cb=b_conv_b[l][None, :],
                   wa=_expand_block_diag(b_wa[l]).astype(BF16), ba=b_ba[l][None, :],
                   wx=_expand_block_diag(b_wx[l]).astype(BF16), bx=b_bx[l][None, :],
                   lam=b_lambda[l][None, :])
        gb_row = jnp.pad(c_gate_b[l], (0, LANES - n_gate))[None, :]
        gb_col = jnp.broadcast_to(jnp.pad(c_gate_b[l], (0, BF16_ROWS - n_gate))[:, None],
                                  (BF16_ROWS, chunk))
        gn = jnp.tile(c_outnorm_g[l], c_heads)[None, :]
        mb = merge_b[l][None, :]
        wb = w_branch[l].astype(BF16)
        wo = w_out[l].astype(BF16)

        u = _inproj(xp, g_norm, w_packed, tn)
        q_t, k_t, v_t, k_out, v_out = _qkv_prompt(u, gq, gk, bd, batch, seq, heads, qk_dim, v_dim, a_col)
        a_out = _attn_prompt(q_t, k_t, v_t, u, a_lambda[l], g_sub, az_col, lam_init, tq=256, tk=512)
        b_out, h_last, tail = _lru_prompt(u, lru, batch, seq, xz_col)
        c_out, c_new, n_new, m_new = _mlstm_prompt(
            u, u[:, off_g:off_g + BF16_ROWS].T, gb_row, gb_col, gn, batch, seq, c_heads, c_hd, chunk, cg_col)
        xp = _merge(xp, a_out, b_out, c_out, u, mg_col, mb, wb, wo)
        outs["kp"].append(k_out.reshape(batch, seq, heads, 2 * qk_dim))
        outs["vp"].append(v_out.reshape(batch, seq, heads, v_dim))
        outs["cp"].append(tail[:, SUBLANES - (conv_w - 1):, :])
        outs["hp"].append(h_last[:, 0, :])
        outs["Cp"].append(c_new)
        outs["np"].append(n_new)
        outs["mp"].append(m_new[:, 0, :c_heads])

        us = _inproj(xs, g_norm, w_packed, tn)
        qn_s, k_s, v_s = _qkv_sample(us, gq, gk, bd, qk_dim, width, a_col)
        a_s = _attn_sample(l, page_table, ck2, cv2, qn_s, k_s, v_s, us[:, off_az:off_az + width],
                           a_lambda[l], g_sub, lam_init)
        b_s, h_s, conv_s = _lru_sample(us, jnp.swapaxes(state_conv[l], 0, 1), state_lru_h[l], lru, xz_col)
        c_s, cs_new, ns_new, ms_new = _mlstm_sample(
            us, gb_row, gn, state_mlstm_C[l], state_mlstm_n[l], state_mlstm_m[l], c_heads, c_hd, cg_col)
        xs = _merge(xs, a_s, b_s, c_s, us, mg_col, mb, wb, wo)
        outs["ks"].append(k_s.reshape(db, 1, heads, 2 * qk_dim))
        outs["vs"].append(v_s.reshape(db, 1, heads, v_dim))
        outs["cs"].append(jnp.swapaxes(conv_s, 0, 1))
        outs["hs"].append(h_s)
        outs["Cs"].append(cs_new)
        outs["ns"].append(ns_new)
        outs["ms"].append(ms_new)

    st = {k: jnp.stack(v) for k, v in outs.items()}
    return (xp.reshape(batch, seq, d), xs.reshape(db, 1, d),
            st["kp"], st["vp"], st["ks"], st["vs"], st["cp"], st["cs"], st["hp"], st["hs"],
            st["Cp"], st["Cs"], st["np"], st["ns"], st["mp"], st["ms"])
```

```python
import functools
import math

import jax
import jax.numpy as jnp
from jax import lax
from jax.experimental import pallas as pl
from jax.experimental.pallas import tpu as pltpu

F32 = jnp.float32
BF16 = jnp.bfloat16

EPS = 1e-6
LRU_C = 8.0
LOG2E = 1.4426950408889634
NEG_INF = float("-inf")

LANES = 128
SUBLANES = 8
BF16_ROWS = 16
MXU_DIM = 256
VMEM_LIMIT = 48 * 1024 * 1024


def _cparams(*sem):
    return pltpu.CompilerParams(dimension_semantics=sem, vmem_limit_bytes=VMEM_LIMIT)


def _sigmoid(x):
    return 1.0 / (1.0 + jnp.exp(-x))


def _silu(x):
    return x * _sigmoid(x)


def _softplus(x):
    return jnp.maximum(x, 0.0) + jnp.log1p(jnp.exp(-jnp.abs(x)))


def _log_sigmoid(x):
    return -_softplus(-x)


def _expm1(x):
    u = jnp.exp(x)
    um1 = u - 1.0
    return jnp.where(um1 == 0.0, x, jnp.where(um1 == -1.0, -1.0, um1 * x / jnp.log(u)))


def _split3(x):
    hi = x.astype(BF16)
    r1 = x - hi.astype(F32)
    mid = r1.astype(BF16)
    lo = (r1 - mid.astype(F32)).astype(BF16)
    return hi, mid, lo


def _group_meansq(x, bd, group):
    x2 = x * x
    hi = x2.astype(BF16)
    lo = (x2 - hi.astype(F32)).astype(BF16)
    cols = []
    for c in range(x.shape[1] // MXU_DIM):
        sl = slice(c * MXU_DIM, (c + 1) * MXU_DIM)
        cols.append(jnp.dot(hi[:, sl], bd, preferred_element_type=F32)
                    + jnp.dot(lo[:, sl], bd, preferred_element_type=F32))
    return jnp.concatenate(cols, axis=1) * (1.0 / group)


def _block_diag_ones(group):
    idx = jnp.arange(MXU_DIM) // group
    return (idx[:, None] == idx[None, :]).astype(BF16)


def _inproj_kernel(x_ref, g_ref, w_ref, o_ref, h_scr):
    @pl.when(pl.program_id(1) == 0)
    def _():
        x = x_ref[...]
        ms = jnp.mean(x * x, axis=-1, keepdims=True)
        h_scr[...] = (x * lax.rsqrt(ms + EPS) * g_ref[...]).astype(BF16)

    o_ref[...] = jnp.dot(h_scr[...], w_ref[...], preferred_element_type=F32)


def _inproj(x, g, w, tn):
    m, d = x.shape
    n = w.shape[1]
    tm = min(m, 1024)
    return pl.pallas_call(
        _inproj_kernel,
        grid=(m // tm, n // tn),
        in_specs=[pl.BlockSpec((tm, d), lambda i, j: (i, 0)),
                  pl.BlockSpec((1, d), lambda i, j: (0, 0)),
                  pl.BlockSpec((d, tn), lambda i, j: (0, j))],
        out_specs=pl.BlockSpec((tm, tn), lambda i, j: (i, j)),
        out_shape=jax.ShapeDtypeStruct((m, n), F32),
        scratch_shapes=[pltpu.VMEM((tm, d), BF16)],
        compiler_params=_cparams("parallel", "arbitrary"),
        name="inproj",
    )(x, g, w)


def _qk_norm(a_ref, gq_ref, gk_ref, bd_ref, qk_dim):
    width = a_ref.shape[1] // 3
    bd = bd_ref[...]
    aq = a_ref[:, 0:width]
    ak = a_ref[:, width:2 * width]
    qn = aq * lax.rsqrt(_group_meansq(aq, bd, qk_dim) + EPS) * gq_ref[...]
    kn = ak * lax.rsqrt(_group_meansq(ak, bd, qk_dim) + EPS) * gk_ref[...]
    return qn, kn, a_ref[:, 2 * width:3 * width]


def _qkv_prompt_kernel(a_ref, gq_ref, gk_ref, bd_ref,
                       q_ref, k_ref, v_ref, kout_ref, vout_ref, *, heads, qk_dim, v_dim):
    qn, kn, v = _qk_norm(a_ref, gq_ref, gk_ref, bd_ref, qk_dim)
    rows = v.shape[0]
    for h in range(heads):
        kout_ref[0, pl.ds(h, rows, stride=heads), :] = kn[:, h * v_dim:(h + 1) * v_dim]
        vout_ref[0, pl.ds(h, rows, stride=heads), :] = v[:, h * v_dim:(h + 1) * v_dim]
    qs = (qn * (qk_dim ** -0.5 * LOG2E)).astype(BF16)
    kb = kn.astype(BF16)
    vb = v.astype(BF16)
    lane = lax.broadcasted_iota(jnp.int32, (rows, MXU_DIM - v_dim), 1)
    ones_col = jnp.where(lane == 0, 1.0, 0.0).astype(BF16)
    for h in range(heads):
        q_ref[0, h] = qs[:, h * 2 * qk_dim:(h + 1) * 2 * qk_dim]
        k_ref[0, h] = kb[:, h * 2 * qk_dim:(h + 1) * 2 * qk_dim]
        v_ref[0, h] = jnp.concatenate([vb[:, h * v_dim:(h + 1) * v_dim], ones_col], axis=1)


def _qkv_prompt(u, gq, gk, bd, batch, seq, heads, qk_dim, v_dim, a_col):
    m = batch * seq
    width = heads * v_dim
    tm = min(seq, 512)
    nt = seq // tm
    kern = functools.partial(_qkv_prompt_kernel, heads=heads, qk_dim=qk_dim, v_dim=v_dim)
    const = lambda shape: pl.BlockSpec(shape, lambda b, i: (0,) * len(shape))
    return pl.pallas_call(
        kern,
        grid=(batch, nt),
        in_specs=[pl.BlockSpec((tm, 3 * width), lambda b, i: (b * nt + i, a_col)),
                  const((1, width)), const((1, width)), const((MXU_DIM, MXU_DIM))],
        out_specs=[pl.BlockSpec((1, heads, tm, 2 * qk_dim), lambda b, i: (b, 0, i, 0)),
                   pl.BlockSpec((1, heads, tm, 2 * qk_dim), lambda b, i: (b, 0, i, 0)),
                   pl.BlockSpec((1, heads, tm, MXU_DIM), lambda b, i: (b, 0, i, 0)),
                   pl.BlockSpec((1, tm * heads, v_dim), lambda b, i: (b, i, 0)),
                   pl.BlockSpec((1, tm * heads, v_dim), lambda b, i: (b, i, 0))],
        out_shape=[jax.ShapeDtypeStruct((batch, heads, seq, 2 * qk_dim), BF16),
                   jax.ShapeDtypeStruct((batch, heads, seq, 2 * qk_dim), BF16),
                   jax.ShapeDtypeStruct((batch, heads, seq, MXU_DIM), BF16),
                   jax.ShapeDtypeStruct((batch, seq * heads, v_dim), F32),
                   jax.ShapeDtypeStruct((batch, seq * heads, v_dim), F32)],
        compiler_params=_cparams("parallel", "parallel"),
        name="qkv_prompt",
    )(u, gq, gk, bd)


def _qkv_sample_kernel(a_ref, gq_ref, gk_ref, bd_ref, q_ref, kout_ref, vout_ref, *, qk_dim):
    qn, kn, v = _qk_norm(a_ref, gq_ref, gk_ref, bd_ref, qk_dim)
    q_ref[...] = qn * (qk_dim ** -0.5 * LOG2E)
    kout_ref[...] = kn
    vout_ref[...] = v


def _qkv_sample(u, gq, gk, bd, qk_dim, width, a_col):
    m = u.shape[0]
    kern = functools.partial(_qkv_sample_kernel, qk_dim=qk_dim)
    row = lambda c: pl.BlockSpec((m, width), lambda i, c=c: (0, c))
    const = lambda shape: pl.BlockSpec(shape, lambda i: (0,) * len(shape))
    return pl.pallas_call(
        kern,
        grid=(1,),
        in_specs=[pl.BlockSpec((m, 3 * width), lambda i: (0, a_col)),
                  const((1, width)), const((1, width)), const((MXU_DIM, MXU_DIM))],
        out_specs=[row(0), row(0), row(0)],
        out_shape=[jax.ShapeDtypeStruct((m, width), F32)] * 3,
        compiler_params=_cparams("arbitrary"),
        name="qkv_sample",
    )(u, gq, gk, bd)


def _diff_lambda(lam_ref, lam_init):
    lp = lam_ref[...]
    s1 = jnp.sum(lp[0:1] * lp[1:2], axis=1, keepdims=True)
    s2 = jnp.sum(lp[2:3] * lp[3:4], axis=1, keepdims=True)
    return jnp.exp(s1) - jnp.exp(s2) + lam_init


def _attn_epilogue(o1, l1, o2, l2, lam, g, az, lam_init):
    o = o1 / l1 - lam * (o2 / l2)
    ms = jnp.mean(o * o, axis=-1, keepdims=True)
    return o * lax.rsqrt(ms + EPS) * g * (1.0 - lam_init) * _silu(az)


def _attn_prompt_kernel(slope_ref, lam_ref, q_ref, k_ref, v_ref, az_ref, g_ref,
                        boff_ref, bdiag_ref, o_ref, qs_scr, m_scr, acc_scr, sa_scr, sb_scr,
                        *, tq, tk, qk_dim, v_dim, lam_init):
    h = pl.program_id(1)
    i = pl.program_id(2)
    ns = tk // tq
    beta = slope_ref[h] * LOG2E
    nblk = tk // LANES

    q_all = q_ref[0, 0]
    lane = lax.broadcasted_iota(jnp.int32, (tq, q_all.shape[1]), 1)
    for s in range(ns):
        q = q_all[s * tq:(s + 1) * tq]
        zero = jnp.zeros_like(q)
        qs_scr[s, 0:tq, :] = jnp.where(lane < qk_dim, q, zero)
        qs_scr[s, tq:2 * tq, :] = jnp.where(lane >= qk_dim, q, zero)
    m_scr[...] = jnp.full(m_scr.shape, NEG_INF, F32)
    acc_scr[...] = jnp.zeros(acc_scr.shape, F32)

    def k_tile(kt):
        return k_ref[0, 0, pl.ds(pl.multiple_of(kt * tk, tk), tk), :]

    def v_tile(kt):
        return v_ref[0, 0, pl.ds(pl.multiple_of(kt * tk, tk), tk), :]

    def scores(buf, kt):
        k = k_tile(kt)
        for s in range(ns):
            buf[s] = lax.dot_general(qs_scr[s], k, (((1,), (1,)), ((), ())),
                                     preferred_element_type=F32)

    def softmax_pv(s, sc_ref, v, bias, shift):
        e_rows = []
        alphas = []
        for c in range(2):
            rows = slice(c * tq, (c + 1) * tq)
            blk = [sc_ref[s, rows, j * LANES:(j + 1) * LANES] + bias[:, j * LANES:(j + 1) * LANES]
                   for j in range(nblk)]
            pm = blk[0]
            for b in blk[1:]:
                pm = jnp.maximum(pm, b)
            r = jnp.max(pm, axis=1, keepdims=True)
            m_old = m_scr[s, rows, :]
            m_new = jnp.maximum(m_old, r + shift)
            d = m_new - shift
            e_rows.append(jnp.concatenate([jnp.exp2(b - d) for b in blk], axis=1).astype(BF16))
            alphas.append(jnp.exp2(m_old - m_new))
            m_scr[s, rows, :] = m_new
        pv = jnp.dot(jnp.concatenate(e_rows, axis=0), v, preferred_element_type=F32)
        alpha = jnp.concatenate(alphas, axis=0)
        for j in range(MXU_DIM // LANES):
            cols = slice(j * LANES, (j + 1) * LANES)
            acc_scr[s, :, cols] = alpha * acc_scr[s, :, cols] + pv[:, cols]

    def full_tile(kt, cur, nxt):
        scores(nxt, kt + 1)
        v = v_tile(kt)
        base = ((kt - i) * tk).astype(F32)
        for s in range(ns):
            softmax_pv(s, cur, v, boff_ref[0], beta * (base - s * tq))

    def diag_tile(cur):
        v = v_tile(i)
        for s in range(ns):
            softmax_pv(s, cur, v, bdiag_ref[0, s], jnp.float32(0.0))

    scores(sa_scr, 0)

    def pair(tt, carry):
        full_tile(2 * tt, sa_scr, sb_scr)
        full_tile(2 * tt + 1, sb_scr, sa_scr)
        return carry

    lax.fori_loop(0, i // 2, pair, 0)

    @pl.when(i % 2 == 1)
    def _():
        full_tile(i - 1, sa_scr, sb_scr)
        diag_tile(sb_scr)

    @pl.when(i % 2 == 0)
    def _():
        diag_tile(sa_scr)

    lam = _diff_lambda(lam_ref, lam_init)
    for s in range(ns):
        acc = acc_scr[s]
        rows = slice(s * tq, (s + 1) * tq)
        o_ref[rows, :] = _attn_epilogue(
            acc[0:tq, 0:v_dim], acc[0:tq, v_dim:v_dim + 1],
            acc[tq:2 * tq, 0:v_dim], acc[tq:2 * tq, v_dim:v_dim + 1],
            lam, g_ref[...], az_ref[rows, :], lam_init).astype(o_ref.dtype)


def _alibi_slopes(heads):
    return jnp.asarray([2.0 ** (-8.0 * (j + 1) / heads) for j in range(heads)], F32)


def _attn_prompt(q, k, v, u, lam_p, g, az_col, lam_init, tq, tk):
    batch, heads, seq, qk2 = q.shape
    qk_dim = qk2 // 2
    v_dim = g.shape[1]
    tq = min(tq, seq)
    tk = min(tk, seq)
    ratio = tk // tq
    nq = seq // tk
    slopes = _alibi_slopes(heads)
    beta = (slopes * LOG2E)[:, None, None]
    rel = (jnp.arange(tk)[None, :] - jnp.arange(tq)[:, None]).astype(F32)
    boff = beta * rel[None]
    offs = (jnp.arange(ratio) * tq).astype(F32)[:, None, None]
    reld = rel[None] - offs
    bdiag = jnp.where(reld[None] <= 0, beta[:, None] * reld[None], NEG_INF)
    kern = functools.partial(_attn_prompt_kernel, tq=tq, tk=tk, qk_dim=qk_dim, v_dim=v_dim,
                             lam_init=lam_init)
    return pl.pallas_call(
        kern,
        grid=(batch, heads, nq),
        in_specs=[pl.BlockSpec(memory_space=pltpu.SMEM),
                  pl.BlockSpec(lam_p.shape, lambda b, h, i: (0, 0)),
                  pl.BlockSpec((1, 1, tk, qk2), lambda b, h, i: (b, h, i, 0)),
                  pl.BlockSpec((1, 1, seq, qk2), lambda b, h, i: (b, h, 0, 0)),
                  pl.BlockSpec((1, 1, seq, MXU_DIM), lambda b, h, i: (b, h, 0, 0)),
                  pl.BlockSpec((tk, v_dim), lambda b, h, i: (b * nq + i, az_col + h)),
                  pl.BlockSpec((1, v_dim), lambda b, h, i: (0, 0)),
                  pl.BlockSpec((1, tq, tk), lambda b, h, i: (h, 0, 0)),
                  pl.BlockSpec((1, ratio, tq, tk), lambda b, h, i: (h, 0, 0, 0))],
        out_specs=pl.BlockSpec((tk, v_dim), lambda b, h, i: (b * nq + i, h)),
        out_shape=jax.ShapeDtypeStruct((batch * seq, heads * v_dim), BF16),
        scratch_shapes=[pltpu.VMEM((ratio, 2 * tq, qk2), BF16),
                        pltpu.VMEM((ratio, 2 * tq, LANES), F32),
                        pltpu.VMEM((ratio, 2 * tq, MXU_DIM), F32),
                        pltpu.VMEM((ratio, 2 * tq, tk), F32),
                        pltpu.VMEM((ratio, 2 * tq, tk), F32)],
        compiler_params=_cparams("parallel", "parallel", "arbitrary"),
        name="attn_prompt",
    )(slopes, lam_p, q, k, v, u, g, boff, bdiag)


def _attn_sample_kernel(pt_ref, lam_ref, q_ref, kx_ref, vx_ref, az_ref, g_ref, beta_ref,
                        bias_ref, *rest, pg, rows, past, heads, v_dim, lam_init):
    k_refs = rest[:pg]
    v_refs = rest[pg:2 * pg]
    o_ref = rest[2 * pg]
    m_scr, l_scr, acc_scr = rest[2 * pg + 1:]
    g_idx = pl.program_id(1)
    n_groups = pl.num_programs(1)
    tokens = rows // heads

    @pl.when(g_idx == 0)
    def _():
        m_scr[...] = jnp.full(m_scr.shape, NEG_INF, F32)
        l_scr[...] = jnp.zeros(l_scr.shape, F32)
        acc_scr[...] = jnp.zeros(acc_scr.shape, F32)

    q = q_ref[0]
    beta = beta_ref[...]

    def update(s, vals):
        r = jnp.max(s, axis=1, keepdims=True)
        m_old = m_scr[...]
        m_new = jnp.maximum(m_old, r)
        e = jnp.exp2(s - m_new)
        alpha = jnp.exp2(m_old - m_new)
        pv = vals(e.astype(BF16))
        l_scr[...] = alpha * l_scr[...] + jnp.sum(e, axis=1, keepdims=True)
        acc_scr[...] = alpha * acc_scr[...] + pv
        m_scr[...] = m_new

    s_parts = []
    for j in range(pg):
        kp = k_refs[j][0, 0].astype(BF16)
        s_parts.append(lax.dot_general(q, kp, (((1,), (1,)), ((), ())),
                                       preferred_element_type=F32))
    s = jnp.concatenate(s_parts, axis=1)
    dist0 = (past - g_idx * (pg * tokens)).astype(F32)
    s = s + bias_ref[...] - beta * dist0

    def past_vals(e):
        out = jnp.zeros((e.shape[0], v_dim), F32)
        for j in range(pg):
            out = out + jnp.dot(e[:, j * rows:(j + 1) * rows], v_refs[j][0, 0].astype(BF16),
                                preferred_element_type=F32)
        return out

    update(s, past_vals)

    @pl.when(g_idx == n_groups - 1)
    def _():
        kx = kx_ref[0].astype(BF16).astype(F32)
        vx = vx_ref[0].astype(BF16).astype(F32)
        sx = jnp.sum(q.astype(F32) * kx, axis=1, keepdims=True)
        m_old = m_scr[...]
        m_new = jnp.maximum(m_old, sx)
        ex = jnp.exp2(sx - m_new)
        alpha = jnp.exp2(m_old - m_new)
        l_scr[...] = alpha * l_scr[...] + ex
        acc_scr[...] = alpha * acc_scr[...] + ex * vx
        m_scr[...] = m_new

        lam = _diff_lambda(lam_ref, lam_init)
        acc = acc_scr[...]
        l = l_scr[...]
        az = az_ref[0]
        outs = []
        for h in range(heads):
            outs.append(_attn_epilogue(
                acc[2 * h:2 * h + 1], l[2 * h:2 * h + 1],
                acc[2 * h + 1:2 * h + 2], l[2 * h + 1:2 * h + 2],
                lam, g_ref[...], az[:, h * v_dim:(h + 1) * v_dim], lam_init))
        o_ref[0] = jnp.concatenate(outs, axis=1).astype(o_ref.dtype)


def _attn_sample(layer, page_table, cache_k, cache_v, qn, k_new, v_new, az, lam_p, g, lam_init):
    db, n_pages = page_table.shape
    rows, v_dim = cache_k.shape[2], cache_k.shape[3]
    heads = qn.shape[1] // v_dim
    qk_dim = v_dim // 2
    tokens = rows // heads
    past = n_pages * tokens
    pg = math.gcd(n_pages, 8)
    n_groups = n_pages // pg
    nrow = 2 * heads

    slopes = _alibi_slopes(heads)
    beta_rows = jnp.repeat(slopes * LOG2E, 2)[:, None]
    q3 = qn.reshape(db, heads, 1, v_dim)
    comp = (jnp.arange(v_dim)[None, :] // qk_dim) == jnp.arange(2)[:, None]
    qmat = jnp.where(comp[None, None], q3, 0.0).reshape(db, nrow, v_dim).astype(BF16)
    kx = jnp.repeat(k_new.reshape(db, heads, v_dim), 2, axis=1)
    vx = jnp.repeat(v_new.reshape(db, heads, v_dim), 2, axis=1)
    lane = jnp.arange(pg * rows)
    row_head = jnp.arange(nrow) // 2
    visible = (lane[None, :] % heads) == row_head[:, None]
    bias = jnp.where(visible, beta_rows * (lane[None, :] // heads).astype(F32), NEG_INF)

    kern = functools.partial(_attn_sample_kernel, pg=pg, rows=rows, past=past, heads=heads,
                             v_dim=v_dim, lam_init=lam_init)
    width = heads * v_dim

    def page_spec(j):
        return pl.BlockSpec((1, 1, rows, v_dim),
                            lambda b, gi, pt, j=j: (layer, pt[b, gi * pg + j], 0, 0))

    const = lambda shape: pl.BlockSpec(shape, lambda b, gi, pt: (0,) * len(shape))
    per_seq = lambda shape: pl.BlockSpec((1,) + shape, lambda b, gi, pt: (b, 0, 0))
    grid_spec = pltpu.PrefetchScalarGridSpec(
        num_scalar_prefetch=1,
        grid=(db, n_groups),
        in_specs=[const(lam_p.shape), per_seq((nrow, v_dim)), per_seq((nrow, v_dim)),
                  per_seq((nrow, v_dim)), per_seq((1, width)), const((1, v_dim)),
                  const((nrow, 1)), const((nrow, pg * rows))]
                 + [page_spec(j) for j in range(pg)] * 2,
        out_specs=per_seq((1, width)),
        scratch_shapes=[pltpu.VMEM((nrow, 1), F32), pltpu.VMEM((nrow, 1), F32),
                        pltpu.VMEM((nrow, v_dim), F32)],
    )
    out = pl.pallas_call(
        kern,
        grid_spec=grid_spec,
        out_shape=jax.ShapeDtypeStruct((db, 1, width), BF16),
        compiler_params=_cparams("parallel", "arbitrary"),
        name="attn_sample",
    )(page_table, lam_p, qmat, kx, vx, az.reshape(db, 1, width), g, beta_rows, bias,
      *([cache_k] * pg), *([cache_v] * pg))
    return out.reshape(db, width)


def _lru_gates(y, wa_ref, ba_ref, wx_ref, bx_ref, lam_ref):
    yb = y.astype(BF16)
    r = _sigmoid(jnp.dot(yb, wa_ref[...], preferred_element_type=F32) + ba_ref[...])
    ig = _sigmoid(jnp.dot(yb, wx_ref[...], preferred_element_type=F32) + bx_ref[...])
    log_a = -LRU_C * r * _softplus(-lam_ref[...])
    a = jnp.exp(log_a)
    u = jnp.sqrt(-_expm1(2.0 * log_a)) * (ig * y)
    return a, u


def _lru_prompt_kernel(xz_ref, cw_ref, cb_ref, wa_ref, ba_ref, wx_ref, bx_ref, lam_ref,
                       o_ref, hlast_ref, tail_ref, tail_scr, h_scr, a_scr, u_scr, hs_scr,
                       *, tt, conv_w, batch):
    t = pl.program_id(0)

    @pl.when(t == 0)
    def _():
        tail_scr[...] = jnp.zeros(tail_scr.shape, F32)
        h_scr[...] = jnp.zeros(h_scr.shape, F32)

    width = xz_ref.shape[2] // 2
    cw = cw_ref[...]
    for b in range(batch):
        x = xz_ref[b, :, 0:width]
        xe = jnp.concatenate([tail_scr[b], x], axis=0)
        y = cb_ref[...] + cw[conv_w - 1:conv_w] * x
        for j in range(1, conv_w):
            y = y + cw[conv_w - 1 - j:conv_w - j] * pltpu.roll(xe, j, 0)[SUBLANES:]
        tail_scr[b] = x[tt - SUBLANES:]
        tail_ref[b] = x[tt - SUBLANES:]
        a, u = _lru_gates(y, wa_ref, ba_ref, wx_ref, bx_ref, lam_ref)
        a_scr[b] = a
        u_scr[b] = u

    def group(gi, hs):
        base = pl.multiple_of(gi * SUBLANES, SUBLANES)
        hs = list(hs)
        for r in range(SUBLANES):
            for b in range(batch):
                hs[b] = a_scr[b, pl.ds(base + r, 1), :] * hs[b] + u_scr[b, pl.ds(base + r, 1), :]
                hs_scr[b, pl.ds(base + r, 1), :] = hs[b]
        return tuple(hs)

    hs = lax.fori_loop(0, tt // SUBLANES, group, tuple(h_scr[b] for b in range(batch)))
    for b in range(batch):
        h_scr[b] = hs[b]
        hlast_ref[b] = hs[b]
        o_ref[b] = (hs_scr[b] * _silu(xz_ref[b, :, width:2 * width])).astype(o_ref.dtype)


def _lru_prompt(u, p, batch, seq, xz_col):
    width = p["cw"].shape[1]
    conv_w = p["cw"].shape[0]
    tt = min(seq, 512)
    nt = seq // tt
    kern = functools.partial(_lru_prompt_kernel, tt=tt, conv_w=conv_w, batch=batch)
    const = lambda a: pl.BlockSpec(a.shape, lambda t: (0,) * a.ndim)
    ws = [p["cw"], p["cb"], p["wa"], p["ba"], p["wx"], p["bx"], p["lam"]]
    out, h_last, tail = pl.pallas_call(
        kern,
        grid=(nt,),
        in_specs=[pl.BlockSpec((batch, tt, 2 * width), lambda t: (0, t, xz_col))]
                 + [const(a) for a in ws],
        out_specs=[pl.BlockSpec((batch, tt, width), lambda t: (0, t, 0)),
                   pl.BlockSpec((batch, 1, width), lambda t: (0, 0, 0)),
                   pl.BlockSpec((batch, SUBLANES, width), lambda t: (0, 0, 0))],
        out_shape=[jax.ShapeDtypeStruct((batch, seq, width), BF16),
                   jax.ShapeDtypeStruct((batch, 1, width), F32),
                   jax.ShapeDtypeStruct((batch, SUBLANES, width), F32)],
        scratch_shapes=[pltpu.VMEM((batch, SUBLANES, width), F32), pltpu.VMEM((batch, 1, width), F32),
                        pltpu.VMEM((batch, tt, width), F32), pltpu.VMEM((batch, tt, width), F32),
                        pltpu.VMEM((batch, tt, width), F32)],
        compiler_params=_cparams("arbitrary"),
        name="lru_prompt",
    )(u.reshape(batch, seq, u.shape[1]), *ws)
    return out.reshape(batch * seq, width), h_last, tail


def _head_norm_gate(hcat, o_gate, z_gate, g_tiled, heads, hd):
    x = _sigmoid(o_gate) * hcat
    parts = []
    for h in range(heads):
        xh = x[:, h * hd:(h + 1) * hd]
        ms = jnp.mean(xh * xh, axis=-1, keepdims=True)
        parts.append(xh * lax.rsqrt(ms + EPS))
    return jnp.concatenate(parts, axis=1) * g_tiled * _silu(z_gate)


def _mlstm_chunk(cg, grow_raw, gbr, gbc, gn, c_ref, n_ref, m_ref, b, tri_lower, tri_upper, causal,
                 heads, hd, chunk):
    width = heads * hd
    gcol = cg[:, 5 * width:5 * width + LANES] + gbr
    grow = grow_raw + gbc
    lf_col = _log_sigmoid(gcol)
    lf_row = _log_sigmoid(grow)
    b_col = sum(jnp.dot(tri_lower, part, preferred_element_type=F32) for part in _split3(lf_col))
    b_row = sum(jnp.dot(part, tri_upper, preferred_element_type=F32) for part in _split3(lf_row))

    q_all = cg[:, 0:width]
    k_all = cg[:, width:2 * width] * (hd ** -0.5)
    v_all = cg[:, 2 * width:3 * width]
    m_all = m_ref[b]
    n_all = n_ref[b]
    h_parts = []
    m_parts = []
    for h in range(heads):
        sl = slice(h * hd, (h + 1) * hd)
        bc = b_col[:, heads + h:heads + h + 1]
        igc = gcol[:, h:h + 1]
        br = b_row[heads + h:heads + h + 1, :]
        igr = grow[h:h + 1, :]
        m_prev = m_all[:, h:h + 1]
        dlog = jnp.where(causal, bc - br + igr, NEG_INF)
        g = bc + m_prev
        m_t = jnp.maximum(g, jnp.max(dlog, axis=1, keepdims=True))
        qh = q_all[:, sl].astype(BF16)
        kf = k_all[:, sl]
        vh = v_all[:, sl].astype(BF16)
        qk = lax.dot_general(qh, kf.astype(BF16), (((1,), (1,)), ((), ())),
                             preferred_element_type=F32)
        w = jnp.exp(dlog - m_t) * qk
        inter = jnp.exp(g - m_t)
        c_old = c_ref[b, h]
        n_old = n_all[h:h + 1, :]
        num = (jnp.dot(w.astype(BF16), vh, preferred_element_type=F32)
               + inter * jnp.dot(qh, c_old.astype(BF16), preferred_element_type=F32))
        den = (jnp.sum(w, axis=1, keepdims=True)
               + inter * jnp.sum(q_all[:, sl] * n_old, axis=1, keepdims=True))
        h_parts.append(num / jnp.maximum(jnp.abs(den), jnp.exp(-m_t)))
        m_new = m_t[chunk - 1:chunk, :]
        b_last = bc[chunk - 1:chunk, :]
        wt = jnp.exp(b_last - bc + igc - m_new)
        decay = jnp.exp(b_last + m_prev - m_new)
        kw = kf * wt
        c_ref[b, h] = decay * c_old + jnp.dot(kw.T.astype(BF16), vh, preferred_element_type=F32)
        n_ref[b, h:h + 1, :] = decay * n_old + jnp.sum(kw, axis=0, keepdims=True)
        m_parts.append(m_new)
    m_ref[b] = jnp.concatenate(m_parts + [jnp.zeros((1, LANES - heads), F32)], axis=1)
    return _head_norm_gate(jnp.concatenate(h_parts, axis=1), cg[:, 3 * width:4 * width],
                           cg[:, 4 * width:5 * width], gn, heads, hd)


def _mlstm_prompt_kernel(cg_ref, gr_ref, gbr_ref, gbc_ref,
                         gn_ref, out_ref, c_ref, n_ref, m_ref, *, heads, hd, chunk, batch):
    @pl.when(pl.program_id(0) == 0)
    def _():
        c_ref[...] = jnp.zeros(c_ref.shape, F32)
        n_ref[...] = jnp.zeros(n_ref.shape, F32)
        m_ref[...] = jnp.zeros(m_ref.shape, F32)

    rows = lax.broadcasted_iota(jnp.int32, (chunk, chunk), 0)
    cols = lax.broadcasted_iota(jnp.int32, (chunk, chunk), 1)
    causal = rows >= cols
    tri_lower = jnp.where(causal, 1.0, 0.0).astype(BF16)
    tri_upper = jnp.where(rows <= cols, 1.0, 0.0).astype(BF16)
    for b in range(batch):
        out_ref[b] = _mlstm_chunk(cg_ref[b], gr_ref[b], gbr_ref[...], gbc_ref[...], gn_ref[...],
                                  c_ref, n_ref, m_ref, b, tri_lower, tri_upper, causal,
                                  heads, hd, chunk).astype(out_ref.dtype)


def _mlstm_prompt(u, gb_row, gb_col, gn, batch, seq, heads, hd, chunk, cg_col, gate_off):
    width = heads * hd
    nc = seq // chunk
    u3 = u.reshape(batch, seq, u.shape[1])
    ugate_t = jnp.swapaxes(u3[:, :, gate_off:gate_off + BF16_ROWS], 1, 2)
    kern = functools.partial(_mlstm_prompt_kernel, heads=heads, hd=hd, chunk=chunk, batch=batch)
    const = lambda a: pl.BlockSpec(a.shape, lambda t: (0,) * a.ndim)
    out, c_new, n_new, m_new = pl.pallas_call(
        kern,
        grid=(nc,),
        in_specs=[pl.BlockSpec((batch, chunk, 6 * width), lambda t: (0, t, cg_col)),
                  pl.BlockSpec((batch, BF16_ROWS, chunk), lambda t: (0, 0, t)),
                  const(gb_row), const(gb_col), const(gn)],
        out_specs=[pl.BlockSpec((batch, chunk, width), lambda t: (0, t, 0)),
                   pl.BlockSpec((batch, heads, hd, hd), lambda t: (0, 0, 0, 0)),
                   pl.BlockSpec((batch, heads, hd), lambda t: (0, 0, 0)),
                   pl.BlockSpec((batch, 1, LANES), lambda t: (0, 0, 0))],
        out_shape=[jax.ShapeDtypeStruct((batch, seq, width), BF16),
                   jax.ShapeDtypeStruct((batch, heads, hd, hd), F32),
                   jax.ShapeDtypeStruct((batch, heads, hd), F32),
                   jax.ShapeDtypeStruct((batch, 1, LANES), F32)],
        compiler_params=_cparams("arbitrary"),
        name="mlstm_prompt",
    )(u3, ugate_t, gb_row, gb_col, gn)
    return out.reshape(batch * seq, width), c_new, n_new, m_new


def _lru_sample_kernel(xz_ref, conv_ref, h0_ref, cw_ref, cb_ref, wa_ref, ba_ref, wx_ref,
                       bx_ref, lam_ref, o_ref, h_ref, convn_ref, *, conv_w):
    width = xz_ref.shape[1] // 2
    x = xz_ref[:, 0:width]
    cw = cw_ref[...]
    y = cb_ref[...] + cw[conv_w - 1:conv_w] * x
    for j in range(conv_w - 1):
        y = y + cw[j:j + 1] * conv_ref[j]
    for j in range(conv_w - 2):
        convn_ref[j] = conv_ref[j + 1]
    convn_ref[conv_w - 2] = x
    a, u = _lru_gates(y, wa_ref, ba_ref, wx_ref, bx_ref, lam_ref)
    h = a * h0_ref[...] + u
    h_ref[...] = h
    o_ref[...] = (h * _silu(xz_ref[:, width:2 * width])).astype(o_ref.dtype)


def _lru_sample(u, conv_t, h0, p, xz_col):
    m = u.shape[0]
    width = p["cw"].shape[1]
    conv_w = p["cw"].shape[0]
    kern = functools.partial(_lru_sample_kernel, conv_w=conv_w)
    row = lambda c: pl.BlockSpec((m, width), lambda i, c=c: (0, c))
    full = lambda a: pl.BlockSpec(a.shape, lambda i: (0,) * a.ndim)
    ws = [p["cw"], p["cb"], p["wa"], p["ba"], p["wx"], p["bx"], p["lam"]]
    return pl.pallas_call(
        kern,
        grid=(1,),
        in_specs=[pl.BlockSpec((m, 2 * width), lambda i: (0, xz_col)), full(conv_t), full(h0)]
                 + [full(a) for a in ws],
        out_specs=[row(0), row(0), full(conv_t)],
        out_shape=[jax.ShapeDtypeStruct((m, width), BF16),
                   jax.ShapeDtypeStruct((m, width), F32),
                   jax.ShapeDtypeStruct(conv_t.shape, F32)],
        compiler_params=_cparams("arbitrary"),
        name="lru_sample",
    )(u, conv_t, h0, *ws)


def _mlstm_sample_kernel(cg_ref, gbr_ref, gn_ref,
                         c_ref, n_ref, m_ref, out_ref, cn_ref, nn_ref, mn_ref, h_scr,
                         *, heads, hd, sb):
    width = heads * hd
    gcol = cg_ref[:, 5 * width:5 * width + LANES] + gbr_ref[...]
    ig = gcol[:, 0:heads]
    lf = _log_sigmoid(gcol[:, heads:2 * heads])
    m_prev = m_ref[...]
    g = lf + m_prev
    m_t = jnp.maximum(g, ig)
    w_in = jnp.exp(ig - m_t)
    inter = jnp.exp(g - m_t)
    floor = jnp.exp(-m_t)
    mn_ref[...] = m_t
    q_all = cg_ref[:, 0:width]
    k_all = cg_ref[:, width:2 * width] * (hd ** -0.5)
    v_all = cg_ref[:, 2 * width:3 * width]
    for h in range(heads):
        sl = slice(h * hd, (h + 1) * hd)
        qk = jnp.sum(q_all[:, sl] * k_all[:, sl], axis=1, keepdims=True)
        for s in range(sb):
            q_row = q_all[s:s + 1, sl]
            k_row = k_all[s:s + 1, sl]
            v_row = v_all[s:s + 1, sl]
            q_col = jnp.broadcast_to(q_row, (hd, hd)).T
            k_col = jnp.broadcast_to(k_row, (hd, hd)).T
            c_old = c_ref[s, h]
            n_old = n_ref[s, h:h + 1, :]
            w_s = w_in[s:s + 1, h:h + 1]
            i_s = inter[s:s + 1, h:h + 1]
            num = w_s * qk[s:s + 1] * v_row + i_s * jnp.sum(q_col * c_old, axis=0, keepdims=True)
            den = w_s * qk[s:s + 1] + i_s * jnp.sum(q_row * n_old, axis=1, keepdims=True)
            h_scr[s:s + 1, sl] = num / jnp.maximum(jnp.abs(den), floor[s:s + 1, h:h + 1])
            cn_ref[s, h] = i_s * c_old + (w_s * k_col) * v_row
            nn_ref[s, h:h + 1, :] = i_s * n_old + w_s * k_row
    out_ref[...] = _head_norm_gate(h_scr[...], cg_ref[:, 3 * width:4 * width],
                                   cg_ref[:, 4 * width:5 * width], gn_ref[...],
                                   heads, hd).astype(out_ref.dtype)


def _mlstm_sample(u, gb_row, gn, c0, n0, m0, heads, hd, cg_col):
    m = u.shape[0]
    width = heads * hd
    sb = SUBLANES
    kern = functools.partial(_mlstm_sample_kernel, heads=heads, hd=hd, sb=sb)
    row = lambda c: pl.BlockSpec((sb, width), lambda i, c=c: (i, c))
    const = lambda a: pl.BlockSpec(a.shape, lambda i: (0,) * a.ndim)
    c_spec = pl.BlockSpec((sb, heads, hd, hd), lambda i: (i, 0, 0, 0))
    n_spec = pl.BlockSpec((sb, heads, hd), lambda i: (i, 0, 0))
    m_spec = pl.BlockSpec((sb, heads), lambda i: (i, 0))
    return pl.pallas_call(
        kern,
        grid=(m // sb,),
        in_specs=[pl.BlockSpec((sb, 6 * width), lambda i: (i, cg_col)), const(gb_row), const(gn),
                  c_spec, n_spec, m_spec],
        out_specs=[row(0), c_spec, n_spec, m_spec],
        out_shape=[jax.ShapeDtypeStruct((m, width), BF16),
                   jax.ShapeDtypeStruct(c0.shape, F32),
                   jax.ShapeDtypeStruct(n0.shape, F32),
                   jax.ShapeDtypeStruct(m0.shape, F32)],
        scratch_shapes=[pltpu.VMEM((sb, width), F32)],
        compiler_params=_cparams("parallel"),
        name="mlstm_sample",
    )(u, gb_row, gn, c0, n0, m0)


def _merge_kernel(x_ref, a_ref, b_ref, c_ref, mg_ref, mb_ref, wb_ref, wo_ref, y_ref, *, d):
    merged = None
    for j, br in enumerate((a_ref, b_ref, c_ref)):
        proj = jnp.dot(br[...], wb_ref[j], preferred_element_type=F32)
        gate = _sigmoid(mg_ref[:, j * d:(j + 1) * d] + mb_ref[:, j * d:(j + 1) * d])
        merged = gate * proj if merged is None else merged + gate * proj
    y_ref[...] = x_ref[...] + jnp.dot(merged.astype(BF16), wo_ref[...],
                                      preferred_element_type=F32)


def _merge(x, a, b, c, u, mg_col, mb, wb, wo):
    m, d = x.shape
    width = a.shape[1]
    tm = min(m, 256)
    kern = functools.partial(_merge_kernel, d=d)
    row = lambda w: pl.BlockSpec((tm, w), lambda i: (i, 0))
    const = lambda arr: pl.BlockSpec(arr.shape, lambda i: (0,) * arr.ndim)
    return pl.pallas_call(
        kern,
        grid=(m // tm,),
        in_specs=[row(d), row(width), row(width), row(width),
                  pl.BlockSpec((tm, mb.shape[1]), lambda i: (i, mg_col)),
                  const(mb), const(wb), const(wo)],
        out_specs=row(d),
        out_shape=jax.ShapeDtypeStruct((m, d), F32),
        compiler_params=_cparams("parallel"),
        name="merge",
    )(x, a, b, c, u, mb, wb, wo)


def _expand_block_diag(w):
    nb, c, _ = w.shape
    eye = jnp.eye(nb, dtype=w.dtype)
    return (eye[:, None, :, None] * w[:, :, None, :]).reshape(nb * c, nb * c)


def kernel(x_prompt, x_sample, cache_k, cache_v, page_table, state_conv, state_lru_h, state_mlstm_C, state_mlstm_n, state_mlstm_m, norm_g, w_in, a_qnorm_g, a_knorm_g, a_lambda, a_subln_g, b_conv_w, b_conv_b, b_wa, b_ba, b_wx, b_bx, b_lambda, c_gate_b, c_outnorm_g, merge_b, w_branch, w_out):
    batch, seq, d = x_prompt.shape
    db = x_sample.shape[0]
    depth = w_in.shape[0]
    heads = cache_k.shape[3]
    v_dim = cache_v.shape[4]
    qk_dim = a_qnorm_g.shape[1]
    a_width = heads * v_dim
    b_width = b_conv_w.shape[2]
    c_heads = state_mlstm_C.shape[2]
    c_hd = state_mlstm_C.shape[3]
    c_width = c_heads * c_hd
    n_branch = w_branch.shape[1]
    chunk = min(seq, 128)
    assert a_width == b_width == c_width, "column-block addressing assumes equal branch widths"
    width = a_width
    n_gate = 2 * c_heads
    n_merge = n_branch * d
    off_c, off_g, off_b, off_az, off_a, off_mg = 0, 5 * width, 6 * width, 8 * width, 9 * width, 12 * width
    n_packed = off_mg + n_merge
    assert off_mg % n_merge == 0 and width % LANES == 0 and n_gate <= BF16_ROWS
    cg_col, xz_col, az_col, a_col, mg_col = 0, off_b // (2 * width), off_az // v_dim, off_a // (3 * width), off_mg // n_merge
    tn = n_packed // 6
    assert n_packed % 6 == 0 and tn % LANES == 0
    src_a, src_b, src_c = 0, 4 * width, 6 * width
    src_g = src_c + 5 * width

    n_phys, page = cache_k.shape[1], cache_k.shape[2]
    ck2 = cache_k.reshape(depth, n_phys, page * heads, 2 * qk_dim)
    cv2 = cache_v.reshape(depth, n_phys, page * heads, v_dim)
    bd = _block_diag_ones(qk_dim)

    xp = x_prompt.reshape(batch * seq, d)
    xs = x_sample.reshape(db, d)
    outs = {name: [] for name in ("kp", "vp", "ks", "vs", "cp", "cs", "hp", "hs",
                                  "Cp", "Cs", "np", "ns", "mp", "ms")}
    conv_w = b_conv_w.shape[1]

    for l in range(depth):
        lam_init = 0.8 - 0.6 * math.exp(-0.3 * l)
        wl = w_in[l]
        w_packed = jnp.concatenate([
            wl[:, src_c:src_c + 5 * width],
            jnp.pad(wl[:, src_g:src_g + n_gate], ((0, 0), (0, width - n_gate))),
            wl[:, src_b:src_b + 2 * width],
            wl[:, src_a + 3 * width:src_a + 4 * width],
            wl[:, src_a:src_a + 3 * width],
            wl[:, src_g + n_gate:]], axis=1).astype(BF16)
        g_norm = norm_g[l][None, :]
        gq = jnp.tile(a_qnorm_g[l], 2 * heads)[None, :]
        gk = jnp.tile(a_knorm_g[l], 2 * heads)[None, :]
        g_sub = a_subln_g[l][None, :]
        lru = dict(cw=b_conv_w[l], cb=b_conv_b[l][None, :],
                   wa=_expand_block_diag(b_wa[l]).astype(BF16), ba=b_ba[l][None, :],
                   wx=_expand_block_diag(b_wx[l]).astype(BF16), bx=b_bx[l][None, :],
                   lam=b_lambda[l][None, :])
        gb_row = jnp.pad(c_gate_b[l], (0, LANES - n_gate))[None, :]
        gb_col = jnp.broadcast_to(jnp.pad(c_gate_b[l], (0, BF16_ROWS - n_gate))[:, None],
                                  (BF16_ROWS, chunk))
        gn = jnp.tile(c_outnorm_g[l], c_heads)[None, :]
        mb = merge_b[l][None, :]
        wb = w_branch[l].astype(BF16)
        wo = w_out[l].astype(BF16)

        u = _inproj(xp, g_norm, w_packed, tn)
        q_t, k_t, v_t, k_out, v_out = _qkv_prompt(u, gq, gk, bd, batch, seq, heads, qk_dim, v_dim, a_col)
        a_out = _attn_prompt(q_t, k_t, v_t, u, a_lambda[l], g_sub, az_col, lam_init, tq=256, tk=512)
        b_out, h_last, tail = _lru_prompt(u, lru, batch, seq, xz_col)
        c_out, c_new, n_new, m_new = _mlstm_prompt(
            u, gb_row, gb_col, gn, batch, seq, c_heads, c_hd, chunk, cg_col, off_g)
        xp = _merge(xp, a_out, b_out, c_out, u, mg_col, mb, wb, wo)
        outs["kp"].append(k_out.reshape(batch, seq, heads, 2 * qk_dim))
        outs["vp"].append(v_out.reshape(batch, seq, heads, v_dim))
        outs["cp"].append(tail[:, SUBLANES - (conv_w - 1):, :])
        outs["hp"].append(h_last[:, 0, :])
        outs["Cp"].append(c_new)
        outs["np"].append(n_new)
        outs["mp"].append(m_new[:, 0, :c_heads])

        us = _inproj(xs, g_norm, w_packed, tn)
        qn_s, k_s, v_s = _qkv_sample(us, gq, gk, bd, qk_dim, width, a_col)
        a_s = _attn_sample(l, page_table, ck2, cv2, qn_s, k_s, v_s, us[:, off_az:off_az + width],
                           a_lambda[l], g_sub, lam_init)
        b_s, h_s, conv_s = _lru_sample(us, jnp.swapaxes(state_conv[l], 0, 1), state_lru_h[l], lru, xz_col)
        c_s, cs_new, ns_new, ms_new = _mlstm_sample(
            us, gb_row, gn, state_mlstm_C[l], state_mlstm_n[l], state_mlstm_m[l], c_heads, c_hd, cg_col)
        xs = _merge(xs, a_s, b_s, c_s, us, mg_col, mb, wb, wo)
        outs["ks"].append(k_s.reshape(db, 1, heads, 2 * qk_dim))
        outs["vs"].append(v_s.reshape(db, 1, heads, v_dim))
        outs["cs"].append(jnp.swapaxes(conv_s, 0, 1))
        outs["hs"].append(h_s)
        outs["Cs"].append(cs_new)
        outs["ns"].append(ns_new)
        outs["ms"].append(ms_new)

    st = {k: jnp.stack(v) for k, v in outs.items()}
    return (xp.reshape(batch, seq, d), xs.reshape(db, 1, d),
            st["kp"], st["vp"], st["ks"], st["vs"], st["cp"], st["cs"], st["hp"], st["hs"],
            st["Cp"], st["Cs"], st["np"], st["ns"], st["mp"], st["ms"])
```

```python
import functools
import math

import jax
import jax.numpy as jnp
from jax import lax
from jax.experimental import pallas as pl
from jax.experimental.pallas import tpu as pltpu

F32 = jnp.float32
BF16 = jnp.bfloat16

EPS = 1e-6
LRU_C = 8.0
LOG2E = 1.4426950408889634
NEG_INF = float("-inf")

LANES = 128
SUBLANES = 8
BF16_ROWS = 16
MXU_DIM = 256
VMEM_LIMIT = 48 * 1024 * 1024


def _cparams(*sem):
    return pltpu.CompilerParams(dimension_semantics=sem, vmem_limit_bytes=VMEM_LIMIT)


def _sigmoid(x):
    return 1.0 / (1.0 + jnp.exp(-x))


def _silu(x):
    return x * _sigmoid(x)


def _softplus(x):
    return jnp.maximum(x, 0.0) + jnp.log1p(jnp.exp(-jnp.abs(x)))


def _log_sigmoid(x):
    return -_softplus(-x)


def _split3(x):
    hi = x.astype(BF16)
    r1 = x - hi.astype(F32)
    mid = r1.astype(BF16)
    lo = (r1 - mid.astype(F32)).astype(BF16)
    return hi, mid, lo


def _group_meansq(x, bd, group):
    x2 = x * x
    hi = x2.astype(BF16)
    lo = (x2 - hi.astype(F32)).astype(BF16)
    cols = []
    for c in range(x.shape[1] // MXU_DIM):
        sl = slice(c * MXU_DIM, (c + 1) * MXU_DIM)
        cols.append(jnp.dot(hi[:, sl], bd, preferred_element_type=F32)
                    + jnp.dot(lo[:, sl], bd, preferred_element_type=F32))
    return jnp.concatenate(cols, axis=1) * (1.0 / group)


def _block_diag_ones(group):
    idx = jnp.arange(MXU_DIM) // group
    return (idx[:, None] == idx[None, :]).astype(BF16)


def _inproj_kernel(x_ref, g_ref, w_ref, o_ref, h_scr):
    @pl.when(pl.program_id(1) == 0)
    def _():
        x = x_ref[...]
        ms = jnp.mean(x * x, axis=-1, keepdims=True)
        h_scr[...] = (x * lax.rsqrt(ms + EPS) * g_ref[...]).astype(BF16)

    o_ref[...] = jnp.dot(h_scr[...], w_ref[...], preferred_element_type=F32).astype(o_ref.dtype)


def _inproj(x, g, w, tn, out_dtype):
    m, d = x.shape
    n = w.shape[1]
    tm = min(m, 1024)
    return pl.pallas_call(
        _inproj_kernel,
        grid=(m // tm, n // tn),
        in_specs=[pl.BlockSpec((tm, d), lambda i, j: (i, 0)),
                  pl.BlockSpec((1, d), lambda i, j: (0, 0)),
                  pl.BlockSpec((d, tn), lambda i, j: (0, j))],
        out_specs=pl.BlockSpec((tm, tn), lambda i, j: (i, j)),
        out_shape=jax.ShapeDtypeStruct((m, n), out_dtype),
        scratch_shapes=[pltpu.VMEM((tm, d), BF16)],
        compiler_params=_cparams("parallel", "arbitrary"),
        name="inproj",
    )(x, g, w)


def _qk_norm(a_ref, gq_ref, gk_ref, bd_ref, qk_dim):
    width = a_ref.shape[1] // 3
    bd = bd_ref[...]
    aq = a_ref[:, 0:width].astype(F32)
    ak = a_ref[:, width:2 * width].astype(F32)
    qn = aq * lax.rsqrt(_group_meansq(aq, bd, qk_dim) + EPS) * gq_ref[...]
    kn = ak * lax.rsqrt(_group_meansq(ak, bd, qk_dim) + EPS) * gk_ref[...]
    return qn, kn, a_ref[:, 2 * width:3 * width].astype(F32)


def _qkv_prompt_kernel(a_ref, gq_ref, gk_ref, bd_ref,
                       q_ref, k_ref, v_ref, kout_ref, vout_ref, *, heads, qk_dim, v_dim):
    qn, kn, v = _qk_norm(a_ref, gq_ref, gk_ref, bd_ref, qk_dim)
    rows = v.shape[0]
    for h in range(heads):
        kout_ref[0, pl.ds(h, rows, stride=heads), :] = kn[:, h * v_dim:(h + 1) * v_dim]
        vout_ref[0, pl.ds(h, rows, stride=heads), :] = v[:, h * v_dim:(h + 1) * v_dim]
    qs = (qn * (qk_dim ** -0.5 * LOG2E)).astype(BF16)
    kb = kn.astype(BF16)
    vb = v.astype(BF16)
    lane = lax.broadcasted_iota(jnp.int32, (rows, MXU_DIM - v_dim), 1)
    ones_col = jnp.where(lane == 0, 1.0, 0.0).astype(BF16)
    for h in range(heads):
        q_ref[0, h] = qs[:, h * 2 * qk_dim:(h + 1) * 2 * qk_dim]
        k_ref[0, h] = kb[:, h * 2 * qk_dim:(h + 1) * 2 * qk_dim]
        v_ref[0, h] = jnp.concatenate([vb[:, h * v_dim:(h + 1) * v_dim], ones_col], axis=1)


def _qkv_prompt(u, gq, gk, bd, batch, seq, heads, qk_dim, v_dim, a_col):
    m = batch * seq
    width = heads * v_dim
    tm = min(seq, 512)
    nt = seq // tm
    kern = functools.partial(_qkv_prompt_kernel, heads=heads, qk_dim=qk_dim, v_dim=v_dim)
    const = lambda shape: pl.BlockSpec(shape, lambda b, i: (0,) * len(shape))
    return pl.pallas_call(
        kern,
        grid=(batch, nt),
        in_specs=[pl.BlockSpec((tm, 3 * width), lambda b, i: (b * nt + i, a_col)),
                  const((1, width)), const((1, width)), const((MXU_DIM, MXU_DIM))],
        out_specs=[pl.BlockSpec((1, heads, tm, 2 * qk_dim), lambda b, i: (b, 0, i, 0)),
                   pl.BlockSpec((1, heads, tm, 2 * qk_dim), lambda b, i: (b, 0, i, 0)),
                   pl.BlockSpec((1, heads, tm, MXU_DIM), lambda b, i: (b, 0, i, 0)),
                   pl.BlockSpec((1, tm * heads, v_dim), lambda b, i: (b, i, 0)),
                   pl.BlockSpec((1, tm * heads, v_dim), lambda b, i: (b, i, 0))],
        out_shape=[jax.ShapeDtypeStruct((batch, heads, seq, 2 * qk_dim), BF16),
                   jax.ShapeDtypeStruct((batch, heads, seq, 2 * qk_dim), BF16),
                   jax.ShapeDtypeStruct((batch, heads, seq, MXU_DIM), BF16),
                   jax.ShapeDtypeStruct((batch, seq * heads, v_dim), F32),
                   jax.ShapeDtypeStruct((batch, seq * heads, v_dim), F32)],
        compiler_params=_cparams("parallel", "parallel"),
        name="qkv_prompt",
    )(u, gq, gk, bd)


def _qkv_sample_kernel(a_ref, gq_ref, gk_ref, bd_ref, q_ref, kout_ref, vout_ref, *, qk_dim):
    qn, kn, v = _qk_norm(a_ref, gq_ref, gk_ref, bd_ref, qk_dim)
    q_ref[...] = qn * (qk_dim ** -0.5 * LOG2E)
    kout_ref[...] = kn
    vout_ref[...] = v


def _qkv_sample(u, gq, gk, bd, qk_dim, width, a_col):
    m = u.shape[0]
    kern = functools.partial(_qkv_sample_kernel, qk_dim=qk_dim)
    row = lambda c: pl.BlockSpec((m, width), lambda i, c=c: (0, c))
    const = lambda shape: pl.BlockSpec(shape, lambda i: (0,) * len(shape))
    return pl.pallas_call(
        kern,
        grid=(1,),
        in_specs=[pl.BlockSpec((m, 3 * width), lambda i: (0, a_col)),
                  const((1, width)), const((1, width)), const((MXU_DIM, MXU_DIM))],
        out_specs=[row(0), row(0), row(0)],
        out_shape=[jax.ShapeDtypeStruct((m, width), F32)] * 3,
        compiler_params=_cparams("arbitrary"),
        name="qkv_sample",
    )(u, gq, gk, bd)


def _diff_lambda(lam_ref, lam_init):
    lp = lam_ref[...]
    s1 = jnp.sum(lp[0:1] * lp[1:2], axis=1, keepdims=True)
    s2 = jnp.sum(lp[2:3] * lp[3:4], axis=1, keepdims=True)
    return jnp.exp(s1) - jnp.exp(s2) + lam_init


def _attn_epilogue(o1, l1, o2, l2, lam, g, az, lam_init):
    o = o1 / l1 - lam * (o2 / l2)
    ms = jnp.mean(o * o, axis=-1, keepdims=True)
    return o * lax.rsqrt(ms + EPS) * g * (1.0 - lam_init) * _silu(az)


def _attn_prompt_kernel(slope_ref, lam_ref, q_ref, k_ref, v_ref, az_ref, g_ref,
                        bias_ref, o_ref, qs_scr, m_scr, acc_scr, sa_scr, sb_scr, pa_scr, pb_scr,
                        *, tq, tk, qk_dim, v_dim, lam_init):
    h = pl.program_id(1)
    i = pl.program_id(2)
    ns = tk // tq
    beta = slope_ref[h] * LOG2E
    nblk = tk // LANES

    q_all = q_ref[0, 0]
    lane = lax.broadcasted_iota(jnp.int32, (tq, q_all.shape[1]), 1)
    for s in range(ns):
        q = q_all[s * tq:(s + 1) * tq]
        zero = jnp.zeros_like(q)
        qs_scr[s, 0:tq, :] = jnp.where(lane < qk_dim, q, zero)
        qs_scr[s, tq:2 * tq, :] = jnp.where(lane >= qk_dim, q, zero)
    m_scr[...] = jnp.full(m_scr.shape, NEG_INF, F32)
    acc_scr[...] = jnp.zeros(acc_scr.shape, F32)

    def k_tile(kt):
        return k_ref[0, 0, pl.ds(pl.multiple_of(kt * tk, tk), tk), :]

    def v_tile(kt):
        return v_ref[0, 0, pl.ds(pl.multiple_of(kt * tk, tk), tk), :]

    def scores(bufs, kt):
        buf, pmax = bufs
        k = k_tile(kt)
        for s in range(ns):
            bias = bias_ref[0, jnp.where(kt == i, 1 + s, 0)]
            sc = lax.dot_general(qs_scr[s], k, (((1,), (1,)), ((), ())), preferred_element_type=F32)
            for c in range(2):
                rows = slice(c * tq, (c + 1) * tq)
                sb = sc[rows] + bias
                buf[s, rows, :] = sb
                pm = sb[:, 0:LANES]
                for j in range(1, nblk):
                    pm = jnp.maximum(pm, sb[:, j * LANES:(j + 1) * LANES])
                pmax[s, rows, :] = pm

    def softmax_pv(s, bufs, v, shift):
        sc_ref, pmax = bufs
        e_rows = []
        alphas = []
        for c in range(2):
            rows = slice(c * tq, (c + 1) * tq)
            r = jnp.max(pmax[s, rows, :], axis=1, keepdims=True)
            m_old = m_scr[s, rows, :]
            m_new = jnp.maximum(m_old, r + shift)
            d = m_new - shift
            e_rows.append(jnp.concatenate(
                [jnp.exp2(sc_ref[s, rows, j * LANES:(j + 1) * LANES] - d) for j in range(nblk)],
                axis=1).astype(BF16))
            alphas.append(jnp.exp2(m_old - m_new))
            m_scr[s, rows, :] = m_new
        pv = jnp.dot(jnp.concatenate(e_rows, axis=0), v, preferred_element_type=F32)
        alpha = jnp.concatenate(alphas, axis=0)
        for j in range(MXU_DIM // LANES):
            cols = slice(j * LANES, (j + 1) * LANES)
            acc_scr[s, :, cols] = alpha * acc_scr[s, :, cols] + pv[:, cols]

    def full_tile(kt, cur, nxt):
        scores(nxt, kt + 1)
        v = v_tile(kt)
        base = ((kt - i) * tk).astype(F32)
        for s in range(ns):
            softmax_pv(s, cur, v, beta * (base - s * tq))

    def diag_tile(cur):
        v = v_tile(i)
        for s in range(ns):
            softmax_pv(s, cur, v, jnp.float32(0.0))

    buf_a = (sa_scr, pa_scr)
    buf_b = (sb_scr, pb_scr)
    scores(buf_a, 0)

    def pair(tt, carry):
        full_tile(2 * tt, buf_a, buf_b)
        full_tile(2 * tt + 1, buf_b, buf_a)
        return carry

    lax.fori_loop(0, i // 2, pair, 0)

    @pl.when(i % 2 == 1)
    def _():
        full_tile(i - 1, buf_a, buf_b)
        diag_tile(buf_b)

    @pl.when(i % 2 == 0)
    def _():
        diag_tile(buf_a)

    lam = _diff_lambda(lam_ref, lam_init)
    for s in range(ns):
        acc = acc_scr[s]
        rows = slice(s * tq, (s + 1) * tq)
        o_ref[rows, :] = _attn_epilogue(
            acc[0:tq, 0:v_dim], acc[0:tq, v_dim:v_dim + 1],
            acc[tq:2 * tq, 0:v_dim], acc[tq:2 * tq, v_dim:v_dim + 1],
            lam, g_ref[...], az_ref[rows, :].astype(F32), lam_init).astype(o_ref.dtype)


def _alibi_slopes(heads):
    return jnp.asarray([2.0 ** (-8.0 * (j + 1) / heads) for j in range(heads)], F32)


def _attn_prompt(q, k, v, u, lam_p, g, az_col, lam_init, tq, tk):
    batch, heads, seq, qk2 = q.shape
    qk_dim = qk2 // 2
    v_dim = g.shape[1]
    tq = min(tq, seq)
    tk = min(tk, seq)
    ratio = tk // tq
    nq = seq // tk
    slopes = _alibi_slopes(heads)
    beta = (slopes * LOG2E)[:, None, None]
    rel = (jnp.arange(tk)[None, :] - jnp.arange(tq)[:, None]).astype(F32)
    boff = beta * rel[None]
    offs = (jnp.arange(ratio) * tq).astype(F32)[:, None, None]
    reld = rel[None] - offs
    bdiag = jnp.where(reld[None] <= 0, beta[:, None] * reld[None], NEG_INF)
    bias = jnp.concatenate([boff[:, None], bdiag], axis=1)
    kern = functools.partial(_attn_prompt_kernel, tq=tq, tk=tk, qk_dim=qk_dim, v_dim=v_dim,
                             lam_init=lam_init)
    return pl.pallas_call(
        kern,
        grid=(batch, heads, nq),
        in_specs=[pl.BlockSpec(memory_space=pltpu.SMEM),
                  pl.BlockSpec(lam_p.shape, lambda b, h, i: (0, 0)),
                  pl.BlockSpec((1, 1, tk, qk2), lambda b, h, i: (b, h, i, 0)),
                  pl.BlockSpec((1, 1, seq, qk2), lambda b, h, i: (b, h, 0, 0)),
                  pl.BlockSpec((1, 1, seq, MXU_DIM), lambda b, h, i: (b, h, 0, 0)),
                  pl.BlockSpec((tk, v_dim), lambda b, h, i: (b * nq + i, az_col + h)),
                  pl.BlockSpec((1, v_dim), lambda b, h, i: (0, 0)),
                  pl.BlockSpec((1, 1 + ratio, tq, tk), lambda b, h, i: (h, 0, 0, 0))],
        out_specs=pl.BlockSpec((tk, v_dim), lambda b, h, i: (b * nq + i, h)),
        out_shape=jax.ShapeDtypeStruct((batch * seq, heads * v_dim), BF16),
        scratch_shapes=[pltpu.VMEM((ratio, 2 * tq, qk2), BF16),
                        pltpu.VMEM((ratio, 2 * tq, LANES), F32),
                        pltpu.VMEM((ratio, 2 * tq, MXU_DIM), F32),
                        pltpu.VMEM((ratio, 2 * tq, tk), F32),
                        pltpu.VMEM((ratio, 2 * tq, tk), F32),
                        pltpu.VMEM((ratio, 2 * tq, LANES), F32),
                        pltpu.VMEM((ratio, 2 * tq, LANES), F32)],
        compiler_params=_cparams("parallel", "parallel", "arbitrary"),
        name="attn_prompt",
    )(slopes, lam_p, q, k, v, u, g, bias)


def _attn_sample_kernel(pt_ref, lam_ref, q_ref, kx_ref, vx_ref, az_ref, g_ref, beta_ref,
                        bias_ref, *rest, pg, rows, past, heads, v_dim, lam_init):
    k_refs = rest[:pg]
    v_refs = rest[pg:2 * pg]
    o_ref = rest[2 * pg]
    m_scr, l_scr, acc_scr = rest[2 * pg + 1:]
    g_idx = pl.program_id(1)
    n_groups = pl.num_programs(1)
    tokens = rows // heads

    @pl.when(g_idx == 0)
    def _():
        m_scr[...] = jnp.full(m_scr.shape, NEG_INF, F32)
        l_scr[...] = jnp.zeros(l_scr.shape, F32)
        acc_scr[...] = jnp.zeros(acc_scr.shape, F32)

    q = q_ref[0]
    beta = beta_ref[...]

    def update(s, vals):
        r = jnp.max(s, axis=1, keepdims=True)
        m_old = m_scr[...]
        m_new = jnp.maximum(m_old, r)
        e = jnp.exp2(s - m_new)
        alpha = jnp.exp2(m_old - m_new)
        pv = vals(e.astype(BF16))
        l_scr[...] = alpha * l_scr[...] + jnp.sum(e, axis=1, keepdims=True)
        acc_scr[...] = alpha * acc_scr[...] + pv
        m_scr[...] = m_new

    s_parts = []
    for j in range(pg):
        kp = k_refs[j][0, 0].astype(BF16)
        s_parts.append(lax.dot_general(q, kp, (((1,), (1,)), ((), ())),
                                       preferred_element_type=F32))
    s = jnp.concatenate(s_parts, axis=1)
    dist0 = (past - g_idx * (pg * tokens)).astype(F32)
    s = s + bias_ref[...] - beta * dist0

    def past_vals(e):
        out = jnp.zeros((e.shape[0], v_dim), F32)
        for j in range(pg):
            out = out + jnp.dot(e[:, j * rows:(j + 1) * rows], v_refs[j][0, 0].astype(BF16),
                                preferred_element_type=F32)
        return out

    update(s, past_vals)

    @pl.when(g_idx == n_groups - 1)
    def _():
        kx = kx_ref[0].astype(BF16).astype(F32)
        vx = vx_ref[0].astype(BF16).astype(F32)
        sx = jnp.sum(q.astype(F32) * kx, axis=1, keepdims=True)
        m_old = m_scr[...]
        m_new = jnp.maximum(m_old, sx)
        ex = jnp.exp2(sx - m_new)
        alpha = jnp.exp2(m_old - m_new)
        l_scr[...] = alpha * l_scr[...] + ex
        acc_scr[...] = alpha * acc_scr[...] + ex * vx
        m_scr[...] = m_new

        lam = _diff_lambda(lam_ref, lam_init)
        acc = acc_scr[...]
        l = l_scr[...]
        az = az_ref[0].astype(F32)
        outs = []
        for h in range(heads):
            outs.append(_attn_epilogue(
                acc[2 * h:2 * h + 1], l[2 * h:2 * h + 1],
                acc[2 * h + 1:2 * h + 2], l[2 * h + 1:2 * h + 2],
                lam, g_ref[...], az[:, h * v_dim:(h + 1) * v_dim], lam_init))
        o_ref[0] = jnp.concatenate(outs, axis=1).astype(o_ref.dtype)


def _attn_sample(layer, page_table, cache_k, cache_v, qn, k_new, v_new, az, lam_p, g, lam_init):
    db, n_pages = page_table.shape
    rows, v_dim = cache_k.shape[2], cache_k.shape[3]
    heads = qn.shape[1] // v_dim
    qk_dim = v_dim // 2
    tokens = rows // heads
    past = n_pages * tokens
    pg = math.gcd(n_pages, 16)
    n_groups = n_pages // pg
    nrow = 2 * heads

    slopes = _alibi_slopes(heads)
    beta_rows = jnp.repeat(slopes * LOG2E, 2)[:, None]
    q3 = qn.reshape(db, heads, 1, v_dim)
    comp = (jnp.arange(v_dim)[None, :] // qk_dim) == jnp.arange(2)[:, None]
    qmat = jnp.where(comp[None, None], q3, 0.0).reshape(db, nrow, v_dim).astype(BF16)
    kx = jnp.repeat(k_new.reshape(db, heads, v_dim), 2, axis=1)
    vx = jnp.repeat(v_new.reshape(db, heads, v_dim), 2, axis=1)
    lane = jnp.arange(pg * rows)
    row_head = jnp.arange(nrow) // 2
    visible = (lane[None, :] % heads) == row_head[:, None]
    bias = jnp.where(visible, beta_rows * (lane[None, :] // heads).astype(F32), NEG_INF)

    kern = functools.partial(_attn_sample_kernel, pg=pg, rows=rows, past=past, heads=heads,
                             v_dim=v_dim, lam_init=lam_init)
    width = heads * v_dim

    def page_spec(j):
        return pl.BlockSpec((1, 1, rows, v_dim),
                            lambda b, gi, pt, j=j: (layer, pt[b, gi * pg + j], 0, 0))

    const = lambda shape: pl.BlockSpec(shape, lambda b, gi, pt: (0,) * len(shape))
    per_seq = lambda shape: pl.BlockSpec((1,) + shape, lambda b, gi, pt: (b, 0, 0))
    grid_spec = pltpu.PrefetchScalarGridSpec(
        num_scalar_prefetch=1,
        grid=(db, n_groups),
        in_specs=[const(lam_p.shape), per_seq((nrow, v_dim)), per_seq((nrow, v_dim)),
                  per_seq((nrow, v_dim)), per_seq((1, width)), const((1, v_dim)),
                  const((nrow, 1)), const((nrow, pg * rows))]
                 + [page_spec(j) for j in range(pg)] * 2,
        out_specs=per_seq((1, width)),
        scratch_shapes=[pltpu.VMEM((nrow, 1), F32), pltpu.VMEM((nrow, 1), F32),
                        pltpu.VMEM((nrow, v_dim), F32)],
    )
    out = pl.pallas_call(
        kern,
        grid_spec=grid_spec,
        out_shape=jax.ShapeDtypeStruct((db, 1, width), BF16),
        compiler_params=_cparams("parallel", "arbitrary"),
        name="attn_sample",
    )(page_table, lam_p, qmat, kx, vx, az.reshape(db, 1, width), g, beta_rows, bias,
      *([cache_k] * pg), *([cache_v] * pg))
    return out.reshape(db, width)


def _lru_gates(y, wa_ref, ba_ref, wx_ref, bx_ref, lam_ref):
    yb = y.astype(BF16)
    r = _sigmoid(jnp.dot(yb, wa_ref[...], preferred_element_type=F32) + ba_ref[...])
    ig = _sigmoid(jnp.dot(yb, wx_ref[...], preferred_element_type=F32) + bx_ref[...])
    log_a = -LRU_C * r * _softplus(-lam_ref[...])
    a = jnp.exp(log_a)
    u = jnp.sqrt((1.0 - a) * (1.0 + a)) * (ig * y)
    return a, u


def _lru_prompt_kernel(xz_ref, cw_ref, cb_ref, wa_ref, ba_ref, wx_ref, bx_ref, lam_ref,
                       o_ref, hlast_ref, tail_ref, tail_scr, h_scr, a_scr, u_scr, hs_scr,
                       *, tt, conv_w, batch):
    t = pl.program_id(0)

    @pl.when(t == 0)
    def _():
        tail_scr[...] = jnp.zeros(tail_scr.shape, F32)
        h_scr[...] = jnp.zeros(h_scr.shape, F32)

    width = xz_ref.shape[2] // 2
    cw = cw_ref[...]
    for b in range(batch):
        x = xz_ref[b, :, 0:width].astype(F32)
        xe = jnp.concatenate([tail_scr[b], x], axis=0)
        y = cb_ref[...] + cw[conv_w - 1:conv_w] * x
        for j in range(1, conv_w):
            y = y + cw[conv_w - 1 - j:conv_w - j] * pltpu.roll(xe, j, 0)[SUBLANES:]
        tail_scr[b] = x[tt - SUBLANES:]
        tail_ref[b] = x[tt - SUBLANES:]
        a, u = _lru_gates(y, wa_ref, ba_ref, wx_ref, bx_ref, lam_ref)
        a_scr[b] = a
        u_scr[b] = u

    def group(gi, hs):
        base = pl.multiple_of(gi * SUBLANES, SUBLANES)
        hs = list(hs)
        for r in range(SUBLANES):
            for b in range(batch):
                hs[b] = a_scr[b, pl.ds(base + r, 1), :] * hs[b] + u_scr[b, pl.ds(base + r, 1), :]
                hs_scr[b, pl.ds(base + r, 1), :] = hs[b]
        return tuple(hs)

    hs = lax.fori_loop(0, tt // SUBLANES, group, tuple(h_scr[b] for b in range(batch)))
    for b in range(batch):
        h_scr[b] = hs[b]
        hlast_ref[b] = hs[b]
        o_ref[b] = (hs_scr[b] * _silu(xz_ref[b, :, width:2 * width].astype(F32))).astype(o_ref.dtype)


def _lru_prompt(u, p, batch, seq, xz_col):
    width = p["cw"].shape[1]
    conv_w = p["cw"].shape[0]
    tt = min(seq, 512)
    nt = seq // tt
    kern = functools.partial(_lru_prompt_kernel, tt=tt, conv_w=conv_w, batch=batch)
    const = lambda a: pl.BlockSpec(a.shape, lambda t: (0,) * a.ndim)
    ws = [p["cw"], p["cb"], p["wa"], p["ba"], p["wx"], p["bx"], p["lam"]]
    out, h_last, tail = pl.pallas_call(
        kern,
        grid=(nt,),
        in_specs=[pl.BlockSpec((batch, tt, 2 * width), lambda t: (0, t, xz_col))]
                 + [const(a) for a in ws],
        out_specs=[pl.BlockSpec((batch, tt, width), lambda t: (0, t, 0)),
                   pl.BlockSpec((batch, 1, width), lambda t: (0, 0, 0)),
                   pl.BlockSpec((batch, SUBLANES, width), lambda t: (0, 0, 0))],
        out_shape=[jax.ShapeDtypeStruct((batch, seq, width), BF16),
                   jax.ShapeDtypeStruct((batch, 1, width), F32),
                   jax.ShapeDtypeStruct((batch, SUBLANES, width), F32)],
        scratch_shapes=[pltpu.VMEM((batch, SUBLANES, width), F32), pltpu.VMEM((batch, 1, width), F32),
                        pltpu.VMEM((batch, tt, width), F32), pltpu.VMEM((batch, tt, width), F32),
                        pltpu.VMEM((batch, tt, width), F32)],
        compiler_params=_cparams("arbitrary"),
        name="lru_prompt",
    )(u.reshape(batch, seq, u.shape[1]), *ws)
    return out.reshape(batch * seq, width), h_last, tail


def _head_norm_gate(hcat, o_gate, z_gate, g_tiled, heads, hd):
    x = _sigmoid(o_gate) * hcat
    parts = []
    for h in range(heads):
        xh = x[:, h * hd:(h + 1) * hd]
        ms = jnp.mean(xh * xh, axis=-1, keepdims=True)
        parts.append(xh * lax.rsqrt(ms + EPS))
    return jnp.concatenate(parts, axis=1) * g_tiled * _silu(z_gate)


def _mlstm_chunk(cg, grow_raw, gbr, gbc, gn, c_ref, n_ref, m_ref, b, tri_lower, tri_upper, causal,
                 heads, hd, chunk):
    width = heads * hd
    gcol = cg[:, 5 * width:5 * width + LANES].astype(F32) + gbr
    grow = grow_raw.astype(F32) + gbc
    lf_col = _log_sigmoid(gcol)
    lf_row = _log_sigmoid(grow)
    b_col = sum(jnp.dot(tri_lower, part, preferred_element_type=F32) for part in _split3(lf_col))
    b_row = sum(jnp.dot(part, tri_upper, preferred_element_type=F32) for part in _split3(lf_row))

    q_all = cg[:, 0:width].astype(F32)
    k_all = cg[:, width:2 * width].astype(F32) * (hd ** -0.5)
    v_all = cg[:, 2 * width:3 * width].astype(F32)
    m_all = m_ref[b]
    n_all = n_ref[b]
    h_parts = []
    m_parts = []
    for h in range(heads):
        sl = slice(h * hd, (h + 1) * hd)
        bc = b_col[:, heads + h:heads + h + 1]
        igc = gcol[:, h:h + 1]
        br = b_row[heads + h:heads + h + 1, :]
        igr = grow[h:h + 1, :]
        m_prev = m_all[:, h:h + 1]
        dlog = jnp.where(causal, bc - br + igr, NEG_INF)
        g = bc + m_prev
        m_t = jnp.maximum(g, jnp.max(dlog, axis=1, keepdims=True))
        qh = q_all[:, sl].astype(BF16)
        kf = k_all[:, sl]
        vh = v_all[:, sl].astype(BF16)
        qk = lax.dot_general(qh, kf.astype(BF16), (((1,), (1,)), ((), ())),
                             preferred_element_type=F32)
        w = jnp.exp(dlog - m_t) * qk
        inter = jnp.exp(g - m_t)
        c_old = c_ref[b, h]
        n_old = n_all[h:h + 1, :]
        num = (jnp.dot(w.astype(BF16), vh, preferred_element_type=F32)
               + inter * jnp.dot(qh, c_old.astype(BF16), preferred_element_type=F32))
        den = (jnp.sum(w, axis=1, keepdims=True)
               + inter * jnp.sum(q_all[:, sl] * n_old, axis=1, keepdims=True))
        h_parts.append(num / jnp.maximum(jnp.abs(den), jnp.exp(-m_t)))
        m_new = m_t[chunk - 1:chunk, :]
        b_last = bc[chunk - 1:chunk, :]
        wt = jnp.exp(b_last - bc + igc - m_new)
        decay = jnp.exp(b_last + m_prev - m_new)
        kw = kf * wt
        c_ref[b, h] = decay * c_old + jnp.dot(kw.T.astype(BF16), vh, preferred_element_type=F32)
        n_ref[b, h:h + 1, :] = decay * n_old + jnp.sum(kw, axis=0, keepdims=True)
        m_parts.append(m_new)
    m_ref[b] = jnp.concatenate(m_parts + [jnp.zeros((1, LANES - heads), F32)], axis=1)
    return _head_norm_gate(jnp.concatenate(h_parts, axis=1), cg[:, 3 * width:4 * width].astype(F32),
                           cg[:, 4 * width:5 * width].astype(F32), gn, heads, hd)


def _mlstm_prompt_kernel(cg_ref, gr_ref, gbr_ref, gbc_ref,
                         gn_ref, out_ref, c_ref, n_ref, m_ref, *, heads, hd, chunk, batch):
    @pl.when(pl.program_id(0) == 0)
    def _():
        c_ref[...] = jnp.zeros(c_ref.shape, F32)
        n_ref[...] = jnp.zeros(n_ref.shape, F32)
        m_ref[...] = jnp.zeros(m_ref.shape, F32)

    rows = lax.broadcasted_iota(jnp.int32, (chunk, chunk), 0)
    cols = lax.broadcasted_iota(jnp.int32, (chunk, chunk), 1)
    causal = rows >= cols
    tri_lower = jnp.where(causal, 1.0, 0.0).astype(BF16)
    tri_upper = jnp.where(rows <= cols, 1.0, 0.0).astype(BF16)
    for b in range(batch):
        out_ref[b] = _mlstm_chunk(cg_ref[b], gr_ref[b], gbr_ref[...], gbc_ref[...], gn_ref[...],
                                  c_ref, n_ref, m_ref, b, tri_lower, tri_upper, causal,
                                  heads, hd, chunk).astype(out_ref.dtype)


def _mlstm_prompt(u, gb_row, gb_col, gn, batch, seq, heads, hd, chunk, cg_col, gate_off):
    width = heads * hd
    nc = seq // chunk
    u3 = u.reshape(batch, seq, u.shape[1])
    ugate_t = jnp.swapaxes(u3[:, :, gate_off:gate_off + BF16_ROWS], 1, 2)
    kern = functools.partial(_mlstm_prompt_kernel, heads=heads, hd=hd, chunk=chunk, batch=batch)
    const = lambda a: pl.BlockSpec(a.shape, lambda t: (0,) * a.ndim)
    out, c_new, n_new, m_new = pl.pallas_call(
        kern,
        grid=(nc,),
        in_specs=[pl.BlockSpec((batch, chunk, 6 * width), lambda t: (0, t, cg_col)),
                  pl.BlockSpec((batch, BF16_ROWS, chunk), lambda t: (0, 0, t)),
                  const(gb_row), const(gb_col), const(gn)],
        out_specs=[pl.BlockSpec((batch, chunk, width), lambda t: (0, t, 0)),
                   pl.BlockSpec((batch, heads, hd, hd), lambda t: (0, 0, 0, 0)),
                   pl.BlockSpec((batch, heads, hd), lambda t: (0, 0, 0)),
                   pl.BlockSpec((batch, 1, LANES), lambda t: (0, 0, 0))],
        out_shape=[jax.ShapeDtypeStruct((batch, seq, width), BF16),
                   jax.ShapeDtypeStruct((batch, heads, hd, hd), F32),
                   jax.ShapeDtypeStruct((batch, heads, hd), F32),
                   jax.ShapeDtypeStruct((batch, 1, LANES), F32)],
        compiler_params=_cparams("arbitrary"),
        name="mlstm_prompt",
    )(u3, ugate_t, gb_row, gb_col, gn)
    return out.reshape(batch * seq, width), c_new, n_new, m_new


def _lru_sample_kernel(xz_ref, conv_ref, h0_ref, cw_ref, cb_ref, wa_ref, ba_ref, wx_ref,
                       bx_ref, lam_ref, o_ref, h_ref, convn_ref, *, conv_w):
    width = xz_ref.shape[1] // 2
    x = xz_ref[:, 0:width].astype(F32)
    cw = cw_ref[...]
    y = cb_ref[...] + cw[conv_w - 1:conv_w] * x
    for j in range(conv_w - 1):
        y = y + cw[j:j + 1] * conv_ref[j]
    for j in range(conv_w - 2):
        convn_ref[j] = conv_ref[j + 1]
    convn_ref[conv_w - 2] = x
    a, u = _lru_gates(y, wa_ref, ba_ref, wx_ref, bx_ref, lam_ref)
    h = a * h0_ref[...] + u
    h_ref[...] = h
    o_ref[...] = (h * _silu(xz_ref[:, width:2 * width].astype(F32))).astype(o_ref.dtype)


def _lru_sample(u, conv_t, h0, p, xz_col):
    m = u.shape[0]
    width = p["cw"].shape[1]
    conv_w = p["cw"].shape[0]
    kern = functools.partial(_lru_sample_kernel, conv_w=conv_w)
    row = lambda c: pl.BlockSpec((m, width), lambda i, c=c: (0, c))
    full = lambda a: pl.BlockSpec(a.shape, lambda i: (0,) * a.ndim)
    ws = [p["cw"], p["cb"], p["wa"], p["ba"], p["wx"], p["bx"], p["lam"]]
    return pl.pallas_call(
        kern,
        grid=(1,),
        in_specs=[pl.BlockSpec((m, 2 * width), lambda i: (0, xz_col)), full(conv_t), full(h0)]
                 + [full(a) for a in ws],
        out_specs=[row(0), row(0), full(conv_t)],
        out_shape=[jax.ShapeDtypeStruct((m, width), BF16),
                   jax.ShapeDtypeStruct((m, width), F32),
                   jax.ShapeDtypeStruct(conv_t.shape, F32)],
        compiler_params=_cparams("arbitrary"),
        name="lru_sample",
    )(u, conv_t, h0, *ws)


def _mlstm_sample_kernel(cg_ref, gbr_ref, gn_ref,
                         c_ref, n_ref, m_ref, out_ref, cn_ref, nn_ref, mn_ref, h_scr,
                         *, heads, hd, sb):
    width = heads * hd
    gcol = cg_ref[:, 5 * width:5 * width + LANES] + gbr_ref[...]
    ig = gcol[:, 0:heads]
    lf = _log_sigmoid(gcol[:, heads:2 * heads])
    m_prev = m_ref[...]
    g = lf + m_prev
    m_t = jnp.maximum(g, ig)
    w_in = jnp.exp(ig - m_t)
    inter = jnp.exp(g - m_t)
    floor = jnp.exp(-m_t)
    mn_ref[...] = m_t
    q_all = cg_ref[:, 0:width]
    k_all = cg_ref[:, width:2 * width] * (hd ** -0.5)
    v_all = cg_ref[:, 2 * width:3 * width]
    for h in range(heads):
        sl = slice(h * hd, (h + 1) * hd)
        qk = jnp.sum(q_all[:, sl] * k_all[:, sl], axis=1, keepdims=True)
        for s in range(sb):
            q_row = q_all[s:s + 1, sl]
            k_row = k_all[s:s + 1, sl]
            v_row = v_all[s:s + 1, sl]
            q_col = jnp.broadcast_to(q_row, (hd, hd)).T
            k_col = jnp.broadcast_to(k_row, (hd, hd)).T
            c_old = c_ref[s, h]
            n_old = n_ref[s, h:h + 1, :]
            w_s = w_in[s:s + 1, h:h + 1]
            i_s = inter[s:s + 1, h:h + 1]
            num = w_s * qk[s:s + 1] * v_row + i_s * jnp.sum(q_col * c_old, axis=0, keepdims=True)
            den = w_s * qk[s:s + 1] + i_s * jnp.sum(q_row * n_old, axis=1, keepdims=True)
            h_scr[s:s + 1, sl] = num / jnp.maximum(jnp.abs(den), floor[s:s + 1, h:h + 1])
            cn_ref[s, h] = i_s * c_old + (w_s * k_col) * v_row
            nn_ref[s, h:h + 1, :] = i_s * n_old + w_s * k_row
    out_ref[...] = _head_norm_gate(h_scr[...], cg_ref[:, 3 * width:4 * width],
                                   cg_ref[:, 4 * width:5 * width], gn_ref[...],
                                   heads, hd).astype(out_ref.dtype)


def _mlstm_sample(u, gb_row, gn, c0, n0, m0, heads, hd, cg_col):
    m = u.shape[0]
    width = heads * hd
    sb = SUBLANES
    kern = functools.partial(_mlstm_sample_kernel, heads=heads, hd=hd, sb=sb)
    row = lambda c: pl.BlockSpec((sb, width), lambda i, c=c: (i, c))
    const = lambda a: pl.BlockSpec(a.shape, lambda i: (0,) * a.ndim)
    c_spec = pl.BlockSpec((sb, heads, hd, hd), lambda i: (i, 0, 0, 0))
    n_spec = pl.BlockSpec((sb, heads, hd), lambda i: (i, 0, 0))
    m_spec = pl.BlockSpec((sb, heads), lambda i: (i, 0))
    return pl.pallas_call(
        kern,
        grid=(m // sb,),
        in_specs=[pl.BlockSpec((sb, 6 * width), lambda i: (i, cg_col)), const(gb_row), const(gn),
                  c_spec, n_spec, m_spec],
        out_specs=[row(0), c_spec, n_spec, m_spec],
        out_shape=[jax.ShapeDtypeStruct((m, width), BF16),
                   jax.ShapeDtypeStruct(c0.shape, F32),
                   jax.ShapeDtypeStruct(n0.shape, F32),
                   jax.ShapeDtypeStruct(m0.shape, F32)],
        scratch_shapes=[pltpu.VMEM((sb, width), F32)],
        compiler_params=_cparams("parallel"),
        name="mlstm_sample",
    )(u, gb_row, gn, c0, n0, m0)


def _merge_kernel(x_ref, a_ref, b_ref, c_ref, mg_ref, mb_ref, wb_ref, wo_ref, y_ref, *, d):
    merged = None
    for j, br in enumerate((a_ref, b_ref, c_ref)):
        proj = jnp.dot(br[...], wb_ref[j], preferred_element_type=F32)
        gate = _sigmoid(mg_ref[:, j * d:(j + 1) * d].astype(F32) + mb_ref[:, j * d:(j + 1) * d])
        merged = gate * proj if merged is None else merged + gate * proj
    y_ref[...] = x_ref[...] + jnp.dot(merged.astype(BF16), wo_ref[...],
                                      preferred_element_type=F32)


def _merge(x, a, b, c, u, mg_col, mb, wb, wo):
    m, d = x.shape
    width = a.shape[1]
    tm = min(m, 512)
    kern = functools.partial(_merge_kernel, d=d)
    row = lambda w: pl.BlockSpec((tm, w), lambda i: (i, 0))
    const = lambda arr: pl.BlockSpec(arr.shape, lambda i: (0,) * arr.ndim)
    return pl.pallas_call(
        kern,
        grid=(m // tm,),
        in_specs=[row(d), row(width), row(width), row(width),
                  pl.BlockSpec((tm, mb.shape[1]), lambda i: (i, mg_col)),
                  const(mb), const(wb), const(wo)],
        out_specs=row(d),
        out_shape=jax.ShapeDtypeStruct((m, d), F32),
        compiler_params=_cparams("parallel"),
        name="merge",
    )(x, a, b, c, u, mb, wb, wo)


def _expand_block_diag(w):
    nb, c, _ = w.shape
    eye = jnp.eye(nb, dtype=w.dtype)
    return (eye[:, None, :, None] * w[:, :, None, :]).reshape(nb * c, nb * c)


def kernel(x_prompt, x_sample, cache_k, cache_v, page_table, state_conv, state_lru_h, state_mlstm_C, state_mlstm_n, state_mlstm_m, norm_g, w_in, a_qnorm_g, a_knorm_g, a_lambda, a_subln_g, b_conv_w, b_conv_b, b_wa, b_ba, b_wx, b_bx, b_lambda, c_gate_b, c_outnorm_g, merge_b, w_branch, w_out):
    batch, seq, d = x_prompt.shape
    db = x_sample.shape[0]
    depth = w_in.shape[0]
    heads = cache_k.shape[3]
    v_dim = cache_v.shape[4]
    qk_dim = a_qnorm_g.shape[1]
    a_width = heads * v_dim
    b_width = b_conv_w.shape[2]
    c_heads = state_mlstm_C.shape[2]
    c_hd = state_mlstm_C.shape[3]
    c_width = c_heads * c_hd
    n_branch = w_branch.shape[1]
    chunk = min(seq, 128)
    assert a_width == b_width == c_width, "column-block addressing assumes equal branch widths"
    width = a_width
    n_gate = 2 * c_heads
    n_merge = n_branch * d
    off_c, off_g, off_b, off_az, off_a, off_mg = 0, 5 * width, 6 * width, 8 * width, 9 * width, 12 * width
    n_packed = off_mg + n_merge
    assert off_mg % n_merge == 0 and width % LANES == 0 and n_gate <= BF16_ROWS
    cg_col, xz_col, az_col, a_col, mg_col = 0, off_b // (2 * width), off_az // v_dim, off_a // (3 * width), off_mg // n_merge
    tn = n_packed // 6
    assert n_packed % 6 == 0 and tn % LANES == 0
    src_a, src_b, src_c = 0, 4 * width, 6 * width
    src_g = src_c + 5 * width

    n_phys, page = cache_k.shape[1], cache_k.shape[2]
    ck2 = cache_k.reshape(depth, n_phys, page * heads, 2 * qk_dim)
    cv2 = cache_v.reshape(depth, n_phys, page * heads, v_dim)
    bd = _block_diag_ones(qk_dim)

    xp = x_prompt.reshape(batch * seq, d)
    xs = x_sample.reshape(db, d)
    outs = {name: [] for name in ("kp", "vp", "ks", "vs", "cp", "cs", "hp", "hs",
                                  "Cp", "Cs", "np", "ns", "mp", "ms")}
    conv_w = b_conv_w.shape[1]

    for l in range(depth):
        lam_init = 0.8 - 0.6 * math.exp(-0.3 * l)
        wl = w_in[l]
        w_packed = jnp.concatenate([
            wl[:, src_c:src_c + 5 * width],
            jnp.pad(wl[:, src_g:src_g + n_gate], ((0, 0), (0, width - n_gate))),
            wl[:, src_b:src_b + 2 * width],
            wl[:, src_a + 3 * width:src_a + 4 * width],
            wl[:, src_a:src_a + 3 * width],
            wl[:, src_g + n_gate:]], axis=1).astype(BF16)
        g_norm = norm_g[l][None, :]
        gq = jnp.tile(a_qnorm_g[l], 2 * heads)[None, :]
        gk = jnp.tile(a_knorm_g[l], 2 * heads)[None, :]
        g_sub = a_subln_g[l][None, :]
        lru = dict(cw=b_conv_w[l], cb=b_conv_b[l][None, :],
                   wa=_expand_block_diag(b_wa[l]).astype(BF16), ba=b_ba[l][None, :],
                   wx=_expand_block_diag(b_wx[l]).astype(BF16), bx=b_bx[l][None, :],
                   lam=b_lambda[l][None, :])
        gb_row = jnp.pad(c_gate_b[l], (0, LANES - n_gate))[None, :]
        gb_col = jnp.broadcast_to(jnp.pad(c_gate_b[l], (0, BF16_ROWS - n_gate))[:, None],
                                  (BF16_ROWS, chunk))
        gn = jnp.tile(c_outnorm_g[l], c_heads)[None, :]
        mb = merge_b[l][None, :]
        wb = w_branch[l].astype(BF16)
        wo = w_out[l].astype(BF16)

        u = _inproj(xp, g_norm, w_packed, tn, BF16)
        q_t, k_t, v_t, k_out, v_out = _qkv_prompt(u, gq, gk, bd, batch, seq, heads, qk_dim, v_dim, a_col)
        a_out = _attn_prompt(q_t, k_t, v_t, u, a_lambda[l], g_sub, az_col, lam_init, tq=256, tk=512)
        b_out, h_last, tail = _lru_prompt(u, lru, batch, seq, xz_col)
        c_out, c_new, n_new, m_new = _mlstm_prompt(
            u, gb_row, gb_col, gn, batch, seq, c_heads, c_hd, chunk, cg_col, off_g)
        xp = _merge(xp, a_out, b_out, c_out, u, mg_col, mb, wb, wo)
        outs["kp"].append(k_out.reshape(batch, seq, heads, 2 * qk_dim))
        outs["vp"].append(v_out.reshape(batch, seq, heads, v_dim))
        outs["cp"].append(tail[:, SUBLANES - (conv_w - 1):, :])
        outs["hp"].append(h_last[:, 0, :])
        outs["Cp"].append(c_new)
        outs["np"].append(n_new)
        outs["mp"].append(m_new[:, 0, :c_heads])

        us = _inproj(xs, g_norm, w_packed, tn, F32)
        qn_s, k_s, v_s = _qkv_sample(us, gq, gk, bd, qk_dim, width, a_col)
        a_s = _attn_sample(l, page_table, ck2, cv2, qn_s, k_s, v_s, us[:, off_az:off_az + width],
                           a_lambda[l], g_sub, lam_init)
        b_s, h_s, conv_s = _lru_sample(us, jnp.swapaxes(state_conv[l], 0, 1), state_lru_h[l], lru, xz_col)
        c_s, cs_new, ns_new, ms_new = _mlstm_sample(
            us, gb_row, gn, state_mlstm_C[l], state_mlstm_n[l], state_mlstm_m[l], c_heads, c_hd, cg_col)
        xs = _merge(xs, a_s, b_s, c_s, us, mg_col, mb, wb, wo)
        outs["ks"].append(k_s.reshape(db, 1, heads, 2 * qk_dim))
        outs["vs"].append(v_s.reshape(db, 1, heads, v_dim))
        outs["cs"].append(jnp.swapaxes(conv_s, 0, 1))
        outs["hs"].append(h_s)
        outs["Cs"].append(cs_new)
        outs["ns"].append(ns_new)
        outs["ms"].append(ms_new)

    st = {k: jnp.stack(v) for k, v in outs.items()}
    return (xp.reshape(batch, seq, d), xs.reshape(db, 1, d),
            st["kp"], st["vp"], st["ks"], st["vs"], st["cp"], st["cs"], st["hp"], st["hs"],
            st["Cp"], st["Cs"], st["np"], st["ns"], st["mp"], st["ms"])
```

```python
import functools
import math

import jax
import jax.numpy as jnp
from jax import lax
from jax.experimental import pallas as pl
from jax.experimental.pallas import tpu as pltpu

F32 = jnp.float32
BF16 = jnp.bfloat16

EPS = 1e-6
LRU_C = 8.0
LOG2E = 1.4426950408889634
NEG_INF = float("-inf")

LANES = 128
SUBLANES = 8
BF16_ROWS = 16
MXU_DIM = 256
VMEM_LIMIT = 48 * 1024 * 1024

INPROJ_TM = 1024
INPROJ_COL_STEPS = 6
QKV_TM = 512
ATTN_TQ = 256
ATTN_TK = 512
LRU_TT = 512
MLSTM_CHUNK = 128
MERGE_TM = 512
DECODE_PAGES = 32


def _cparams(*sem):
    return pltpu.CompilerParams(dimension_semantics=sem, vmem_limit_bytes=VMEM_LIMIT)


def _sigmoid(x):
    return 1.0 / (1.0 + jnp.exp(-x))


def _silu(x):
    return x * _sigmoid(x)


def _softplus(x):
    return jnp.maximum(x, 0.0) + jnp.log1p(jnp.exp(-jnp.abs(x)))


def _log_sigmoid(x):
    return -_softplus(-x)


def _split3(x):
    hi = x.astype(BF16)
    r1 = x - hi.astype(F32)
    mid = r1.astype(BF16)
    lo = (r1 - mid.astype(F32)).astype(BF16)
    return hi, mid, lo


def _group_meansq(x, bd, group):
    x2 = x * x
    hi = x2.astype(BF16)
    lo = (x2 - hi.astype(F32)).astype(BF16)
    cols = []
    for c in range(x.shape[1] // MXU_DIM):
        sl = slice(c * MXU_DIM, (c + 1) * MXU_DIM)
        cols.append(jnp.dot(hi[:, sl], bd, preferred_element_type=F32)
                    + jnp.dot(lo[:, sl], bd, preferred_element_type=F32))
    return jnp.concatenate(cols, axis=1) * (1.0 / group)


def _block_diag_ones(group):
    idx = jnp.arange(MXU_DIM) // group
    return (idx[:, None] == idx[None, :]).astype(BF16)


def _inproj_kernel(x_ref, g_ref, w_ref, o_ref, h_scr):
    @pl.when(pl.program_id(1) == 0)
    def _():
        x = x_ref[...]
        ms = jnp.mean(x * x, axis=-1, keepdims=True)
        h_scr[...] = (x * lax.rsqrt(ms + EPS) * g_ref[...]).astype(BF16)

    o_ref[...] = jnp.dot(h_scr[...], w_ref[...], preferred_element_type=F32).astype(o_ref.dtype)


def _inproj(x, g, w, tn, out_dtype):
    m, d = x.shape
    n = w.shape[1]
    tm = min(m, INPROJ_TM)
    return pl.pallas_call(
        _inproj_kernel,
        grid=(m // tm, n // tn),
        in_specs=[pl.BlockSpec((tm, d), lambda i, j: (i, 0)),
                  pl.BlockSpec((1, d), lambda i, j: (0, 0)),
                  pl.BlockSpec((d, tn), lambda i, j: (0, j))],
        out_specs=pl.BlockSpec((tm, tn), lambda i, j: (i, j)),
        out_shape=jax.ShapeDtypeStruct((m, n), out_dtype),
        scratch_shapes=[pltpu.VMEM((tm, d), BF16)],
        compiler_params=_cparams("parallel", "arbitrary"),
        name="inproj",
    )(x, g, w)


def _qk_norm(a_ref, gq_ref, gk_ref, bd_ref, qk_dim):
    width = a_ref.shape[1] // 3
    bd = bd_ref[...]
    aq = a_ref[:, 0:width].astype(F32)
    ak = a_ref[:, width:2 * width].astype(F32)
    qn = aq * lax.rsqrt(_group_meansq(aq, bd, qk_dim) + EPS) * gq_ref[...]
    kn = ak * lax.rsqrt(_group_meansq(ak, bd, qk_dim) + EPS) * gk_ref[...]
    return qn, kn, a_ref[:, 2 * width:3 * width].astype(F32)


def _qkv_prompt_kernel(a_ref, gq_ref, gk_ref, bd_ref,
                       q_ref, k_ref, v_ref, kout_ref, vout_ref, *, heads, qk_dim, v_dim):
    qn, kn, v = _qk_norm(a_ref, gq_ref, gk_ref, bd_ref, qk_dim)
    rows = v.shape[0]
    for h in range(heads):
        kout_ref[0, pl.ds(h, rows, stride=heads), :] = kn[:, h * v_dim:(h + 1) * v_dim]
        vout_ref[0, pl.ds(h, rows, stride=heads), :] = v[:, h * v_dim:(h + 1) * v_dim]
    qs = (qn * (qk_dim ** -0.5 * LOG2E)).astype(BF16)
    kb = kn.astype(BF16)
    vb = v.astype(BF16)
    lane = lax.broadcasted_iota(jnp.int32, (rows, MXU_DIM - v_dim), 1)
    ones_col = jnp.where(lane == 0, 1.0, 0.0).astype(BF16)
    for h in range(heads):
        q_ref[0, h] = qs[:, h * 2 * qk_dim:(h + 1) * 2 * qk_dim]
        k_ref[0, h] = kb[:, h * 2 * qk_dim:(h + 1) * 2 * qk_dim]
        v_ref[0, h] = jnp.concatenate([vb[:, h * v_dim:(h + 1) * v_dim], ones_col], axis=1)


def _qkv_prompt(u, gq, gk, bd, batch, seq, heads, qk_dim, v_dim, a_col):
    m = batch * seq
    width = heads * v_dim
    tm = min(seq, QKV_TM)
    nt = seq // tm
    kern = functools.partial(_qkv_prompt_kernel, heads=heads, qk_dim=qk_dim, v_dim=v_dim)
    const = lambda shape: pl.BlockSpec(shape, lambda b, i: (0,) * len(shape))
    return pl.pallas_call(
        kern,
        grid=(batch, nt),
        in_specs=[pl.BlockSpec((tm, 3 * width), lambda b, i: (b * nt + i, a_col)),
                  const((1, width)), const((1, width)), const((MXU_DIM, MXU_DIM))],
        out_specs=[pl.BlockSpec((1, heads, tm, 2 * qk_dim), lambda b, i: (b, 0, i, 0)),
                   pl.BlockSpec((1, heads, tm, 2 * qk_dim), lambda b, i: (b, 0, i, 0)),
                   pl.BlockSpec((1, heads, tm, MXU_DIM), lambda b, i: (b, 0, i, 0)),
                   pl.BlockSpec((1, tm * heads, v_dim), lambda b, i: (b, i, 0)),
                   pl.BlockSpec((1, tm * heads, v_dim), lambda b, i: (b, i, 0))],
        out_shape=[jax.ShapeDtypeStruct((batch, heads, seq, 2 * qk_dim), BF16),
                   jax.ShapeDtypeStruct((batch, heads, seq, 2 * qk_dim), BF16),
                   jax.ShapeDtypeStruct((batch, heads, seq, MXU_DIM), BF16),
                   jax.ShapeDtypeStruct((batch, seq * heads, v_dim), F32),
                   jax.ShapeDtypeStruct((batch, seq * heads, v_dim), F32)],
        compiler_params=_cparams("parallel", "parallel"),
        name="qkv_prompt",
    )(u, gq, gk, bd)


def _qkv_sample_kernel(a_ref, gq_ref, gk_ref, bd_ref, q_ref, kout_ref, vout_ref, *, qk_dim):
    qn, kn, v = _qk_norm(a_ref, gq_ref, gk_ref, bd_ref, qk_dim)
    q_ref[...] = qn * (qk_dim ** -0.5 * LOG2E)
    kout_ref[...] = kn
    vout_ref[...] = v


def _qkv_sample(u, gq, gk, bd, qk_dim, width, a_col):
    m = u.shape[0]
    kern = functools.partial(_qkv_sample_kernel, qk_dim=qk_dim)
    row = lambda c: pl.BlockSpec((m, width), lambda i, c=c: (0, c))
    const = lambda shape: pl.BlockSpec(shape, lambda i: (0,) * len(shape))
    return pl.pallas_call(
        kern,
        grid=(1,),
        in_specs=[pl.BlockSpec((m, 3 * width), lambda i: (0, a_col)),
                  const((1, width)), const((1, width)), const((MXU_DIM, MXU_DIM))],
        out_specs=[row(0), row(0), row(0)],
        out_shape=[jax.ShapeDtypeStruct((m, width), F32)] * 3,
        compiler_params=_cparams("arbitrary"),
        name="qkv_sample",
    )(u, gq, gk, bd)


def _diff_lambda(lam_ref, lam_init):
    lp = lam_ref[...]
    s1 = jnp.sum(lp[0:1] * lp[1:2], axis=1, keepdims=True)
    s2 = jnp.sum(lp[2:3] * lp[3:4], axis=1, keepdims=True)
    return jnp.exp(s1) - jnp.exp(s2) + lam_init


def _attn_epilogue(o1, l1, o2, l2, lam, g, az, lam_init):
    o = o1 / l1 - lam * (o2 / l2)
    ms = jnp.mean(o * o, axis=-1, keepdims=True)
    return o * lax.rsqrt(ms + EPS) * g * (1.0 - lam_init) * _silu(az)


def _attn_prompt_kernel(slope_ref, lam_ref, q_ref, k_ref, v_ref, az_ref, g_ref,
                        bias_ref, o_ref, qs_scr, m_scr, acc_scr, sa_scr, sb_scr, pa_scr, pb_scr,
                        *, tq, tk, qk_dim, v_dim, lam_init):
    h = pl.program_id(1)
    i = pl.program_id(2)
    ns = tk // tq
    beta = slope_ref[h] * LOG2E
    nblk = tk // LANES

    q_all = q_ref[0, 0]
    lane = lax.broadcasted_iota(jnp.int32, (tq, q_all.shape[1]), 1)
    for s in range(ns):
        q = q_all[s * tq:(s + 1) * tq]
        zero = jnp.zeros_like(q)
        qs_scr[s, 0:tq, :] = jnp.where(lane < qk_dim, q, zero)
        qs_scr[s, tq:2 * tq, :] = jnp.where(lane >= qk_dim, q, zero)
    m_scr[...] = jnp.full(m_scr.shape, NEG_INF, F32)
    acc_scr[...] = jnp.zeros(acc_scr.shape, F32)

    def k_tile(kt):
        return k_ref[0, 0, pl.ds(pl.multiple_of(kt * tk, tk), tk), :]

    def v_tile(kt):
        return v_ref[0, 0, pl.ds(pl.multiple_of(kt * tk, tk), tk), :]

    def scores(bufs, kt):
        buf, pmax = bufs
        k = k_tile(kt)
        for s in range(ns):
            bias = bias_ref[0, jnp.where(kt == i, 1 + s, 0)]
            sc = lax.dot_general(qs_scr[s], k, (((1,), (1,)), ((), ())), preferred_element_type=F32)
            for c in range(2):
                rows = slice(c * tq, (c + 1) * tq)
                sb = sc[rows] + bias
                buf[s, rows, :] = sb
                pm = sb[:, 0:LANES]
                for j in range(1, nblk):
                    pm = jnp.maximum(pm, sb[:, j * LANES:(j + 1) * LANES])
                pmax[s, rows, :] = pm

    def softmax_pv(s, bufs, v, shift):
        sc_ref, pmax = bufs
        e_rows = []
        alphas = []
        for c in range(2):
            rows = slice(c * tq, (c + 1) * tq)
            r = jnp.max(pmax[s, rows, :], axis=1, keepdims=True)
            m_old = m_scr[s, rows, :]
            m_new = jnp.maximum(m_old, r + shift)
            d = m_new - shift
            e_rows.append(jnp.concatenate(
                [jnp.exp2(sc_ref[s, rows, j * LANES:(j + 1) * LANES] - d) for j in range(nblk)],
                axis=1).astype(BF16))
            alphas.append(jnp.exp2(m_old - m_new))
            m_scr[s, rows, :] = m_new
        pv = jnp.dot(jnp.concatenate(e_rows, axis=0), v, preferred_element_type=F32)
        alpha = jnp.concatenate(alphas, axis=0)
        for j in range(MXU_DIM // LANES):
            cols = slice(j * LANES, (j + 1) * LANES)
            acc_scr[s, :, cols] = alpha * acc_scr[s, :, cols] + pv[:, cols]

    def full_tile(kt, cur, nxt):
        scores(nxt, kt + 1)
        v = v_tile(kt)
        base = ((kt - i) * tk).astype(F32)
        for s in range(ns):
            softmax_pv(s, cur, v, beta * (base - s * tq))

    def diag_tile(cur):
        v = v_tile(i)
        for s in range(ns):
            softmax_pv(s, cur, v, jnp.float32(0.0))

    buf_a = (sa_scr, pa_scr)
    buf_b = (sb_scr, pb_scr)
    scores(buf_a, 0)

    def pair(tt, carry):
        full_tile(2 * tt, buf_a, buf_b)
        full_tile(2 * tt + 1, buf_b, buf_a)
        return carry

    lax.fori_loop(0, i // 2, pair, 0)

    @pl.when(i % 2 == 1)
    def _():
        full_tile(i - 1, buf_a, buf_b)
        diag_tile(buf_b)

    @pl.when(i % 2 == 0)
    def _():
        diag_tile(buf_a)

    lam = _diff_lambda(lam_ref, lam_init)
    for s in range(ns):
        acc = acc_scr[s]
        rows = slice(s * tq, (s + 1) * tq)
        o_ref[rows, :] = _attn_epilogue(
            acc[0:tq, 0:v_dim], acc[0:tq, v_dim:v_dim + 1],
            acc[tq:2 * tq, 0:v_dim], acc[tq:2 * tq, v_dim:v_dim + 1],
            lam, g_ref[...], az_ref[rows, :].astype(F32), lam_init).astype(o_ref.dtype)


def _alibi_slopes(heads):
    return jnp.asarray([2.0 ** (-8.0 * (j + 1) / heads) for j in range(heads)], F32)


def _attn_prompt(q, k, v, u, lam_p, g, az_col, lam_init, tq=ATTN_TQ, tk=ATTN_TK):
    batch, heads, seq, qk2 = q.shape
    qk_dim = qk2 // 2
    v_dim = g.shape[1]
    tq = min(tq, seq)
    tk = min(tk, seq)
    ratio = tk // tq
    nq = seq // tk
    slopes = _alibi_slopes(heads)
    beta = (slopes * LOG2E)[:, None, None]
    rel = (jnp.arange(tk)[None, :] - jnp.arange(tq)[:, None]).astype(F32)
    boff = beta * rel[None]
    offs = (jnp.arange(ratio) * tq).astype(F32)[:, None, None]
    reld = rel[None] - offs
    bdiag = jnp.where(reld[None] <= 0, beta[:, None] * reld[None], NEG_INF)
    bias = jnp.concatenate([boff[:, None], bdiag], axis=1)
    kern = functools.partial(_attn_prompt_kernel, tq=tq, tk=tk, qk_dim=qk_dim, v_dim=v_dim,
                             lam_init=lam_init)
    return pl.pallas_call(
        kern,
        grid=(batch, heads, nq),
        in_specs=[pl.BlockSpec(memory_space=pltpu.SMEM),
                  pl.BlockSpec(lam_p.shape, lambda b, h, i: (0, 0)),
                  pl.BlockSpec((1, 1, tk, qk2), lambda b, h, i: (b, h, i, 0)),
                  pl.BlockSpec((1, 1, seq, qk2), lambda b, h, i: (b, h, 0, 0)),
                  pl.BlockSpec((1, 1, seq, MXU_DIM), lambda b, h, i: (b, h, 0, 0)),
                  pl.BlockSpec((tk, v_dim), lambda b, h, i: (b * nq + i, az_col + h)),
                  pl.BlockSpec((1, v_dim), lambda b, h, i: (0, 0)),
                  pl.BlockSpec((1, 1 + ratio, tq, tk), lambda b, h, i: (h, 0, 0, 0))],
        out_specs=pl.BlockSpec((tk, v_dim), lambda b, h, i: (b * nq + i, h)),
        out_shape=jax.ShapeDtypeStruct((batch * seq, heads * v_dim), BF16),
        scratch_shapes=[pltpu.VMEM((ratio, 2 * tq, qk2), BF16),
                        pltpu.VMEM((ratio, 2 * tq, LANES), F32),
                        pltpu.VMEM((ratio, 2 * tq, MXU_DIM), F32),
                        pltpu.VMEM((ratio, 2 * tq, tk), F32),
                        pltpu.VMEM((ratio, 2 * tq, tk), F32),
                        pltpu.VMEM((ratio, 2 * tq, LANES), F32),
                        pltpu.VMEM((ratio, 2 * tq, LANES), F32)],
        compiler_params=_cparams("parallel", "parallel", "arbitrary"),
        name="attn_prompt",
    )(slopes, lam_p, q, k, v, u, g, bias)


def _attn_sample_kernel(pt_ref, lam_ref, q_ref, kx_ref, vx_ref, az_ref, g_ref, beta_ref,
                        bias_ref, *rest, pg, rows, past, heads, v_dim, lam_init):
    k_refs = rest[:pg]
    v_refs = rest[pg:2 * pg]
    o_ref = rest[2 * pg]
    m_scr, l_scr, acc_scr = rest[2 * pg + 1:]
    g_idx = pl.program_id(1)
    n_groups = pl.num_programs(1)
    tokens = rows // heads

    @pl.when(g_idx == 0)
    def _():
        m_scr[...] = jnp.full(m_scr.shape, NEG_INF, F32)
        l_scr[...] = jnp.zeros(l_scr.shape, F32)
        acc_scr[...] = jnp.zeros(acc_scr.shape, F32)

    q = q_ref[0]
    beta = beta_ref[...]

    def update(s, vals):
        r = jnp.max(s, axis=1, keepdims=True)
        m_old = m_scr[...]
        m_new = jnp.maximum(m_old, r)
        e = jnp.exp2(s - m_new)
        alpha = jnp.exp2(m_old - m_new)
        pv = vals(e.astype(BF16))
        l_scr[...] = alpha * l_scr[...] + jnp.sum(e, axis=1, keepdims=True)
        acc_scr[...] = alpha * acc_scr[...] + pv
        m_scr[...] = m_new

    s_parts = []
    for j in range(pg):
        kp = k_refs[j][0, 0].astype(BF16)
        s_parts.append(lax.dot_general(q, kp, (((1,), (1,)), ((), ())),
                                       preferred_element_type=F32))
    s = jnp.concatenate(s_parts, axis=1)
    dist0 = (past - g_idx * (pg * tokens)).astype(F32)
    s = s + bias_ref[...] - beta * dist0

    def past_vals(e):
        out = jnp.zeros((e.shape[0], v_dim), F32)
        for j in range(pg):
            out = out + jnp.dot(e[:, j * rows:(j + 1) * rows], v_refs[j][0, 0].astype(BF16),
                                preferred_element_type=F32)
        return out

    update(s, past_vals)

    @pl.when(g_idx == n_groups - 1)
    def _():
        kx = kx_ref[0].astype(BF16).astype(F32)
        vx = vx_ref[0].astype(BF16).astype(F32)
        sx = jnp.sum(q.astype(F32) * kx, axis=1, keepdims=True)
        m_old = m_scr[...]
        m_new = jnp.maximum(m_old, sx)
        ex = jnp.exp2(sx - m_new)
        alpha = jnp.exp2(m_old - m_new)
        l_scr[...] = alpha * l_scr[...] + ex
        acc_scr[...] = alpha * acc_scr[...] + ex * vx
        m_scr[...] = m_new

        lam = _diff_lambda(lam_ref, lam_init)
        acc = acc_scr[...]
        l = l_scr[...]
        az = az_ref[0].astype(F32)
        outs = []
        for h in range(heads):
            outs.append(_attn_epilogue(
                acc[2 * h:2 * h + 1], l[2 * h:2 * h + 1],
                acc[2 * h + 1:2 * h + 2], l[2 * h + 1:2 * h + 2],
                lam, g_ref[...], az[:, h * v_dim:(h + 1) * v_dim], lam_init))
        o_ref[0] = jnp.concatenate(outs, axis=1).astype(o_ref.dtype)


def _attn_sample(layer, page_table, cache_k, cache_v, qn, k_new, v_new, az, lam_p, g, lam_init):
    db, n_pages = page_table.shape
    rows, v_dim = cache_k.shape[2], cache_k.shape[3]
    heads = qn.shape[1] // v_dim
    qk_dim = v_dim // 2
    tokens = rows // heads
    past = n_pages * tokens
    pg = math.gcd(n_pages, DECODE_PAGES)
    n_groups = n_pages // pg
    nrow = 2 * heads

    slopes = _alibi_slopes(heads)
    beta_rows = jnp.repeat(slopes * LOG2E, 2)[:, None]
    q3 = qn.reshape(db, heads, 1, v_dim)
    comp = (jnp.arange(v_dim)[None, :] // qk_dim) == jnp.arange(2)[:, None]
    qmat = jnp.where(comp[None, None], q3, 0.0).reshape(db, nrow, v_dim).astype(BF16)
    kx = jnp.repeat(k_new.reshape(db, heads, v_dim), 2, axis=1)
    vx = jnp.repeat(v_new.reshape(db, heads, v_dim), 2, axis=1)
    lane = jnp.arange(pg * rows)
    row_head = jnp.arange(nrow) // 2
    visible = (lane[None, :] % heads) == row_head[:, None]
    bias = jnp.where(visible, beta_rows * (lane[None, :] // heads).astype(F32), NEG_INF)

    kern = functools.partial(_attn_sample_kernel, pg=pg, rows=rows, past=past, heads=heads,
                             v_dim=v_dim, lam_init=lam_init)
    width = heads * v_dim

    def page_spec(j):
        return pl.BlockSpec((1, 1, rows, v_dim),
                            lambda b, gi, pt, j=j: (layer, pt[b, gi * pg + j], 0, 0))

    const = lambda shape: pl.BlockSpec(shape, lambda b, gi, pt: (0,) * len(shape))
    per_seq = lambda shape: pl.BlockSpec((1,) + shape, lambda b, gi, pt: (b, 0, 0))
    grid_spec = pltpu.PrefetchScalarGridSpec(
        num_scalar_prefetch=1,
        grid=(db, n_groups),
        in_specs=[const(lam_p.shape), per_seq((nrow, v_dim)), per_seq((nrow, v_dim)),
                  per_seq((nrow, v_dim)), per_seq((1, width)), const((1, v_dim)),
                  const((nrow, 1)), const((nrow, pg * rows))]
                 + [page_spec(j) for j in range(pg)] * 2,
        out_specs=per_seq((1, width)),
        scratch_shapes=[pltpu.VMEM((nrow, 1), F32), pltpu.VMEM((nrow, 1), F32),
                        pltpu.VMEM((nrow, v_dim), F32)],
    )
    out = pl.pallas_call(
        kern,
        grid_spec=grid_spec,
        out_shape=jax.ShapeDtypeStruct((db, 1, width), BF16),
        compiler_params=_cparams("parallel", "arbitrary"),
        name="attn_sample",
    )(page_table, lam_p, qmat, kx, vx, az.reshape(db, 1, width), g, beta_rows, bias,
      *([cache_k] * pg), *([cache_v] * pg))
    return out.reshape(db, width)


def _lru_gates(y, wa_ref, ba_ref, wx_ref, bx_ref, lam_ref):
    yb = y.astype(BF16)
    r = _sigmoid(jnp.dot(yb, wa_ref[...], preferred_element_type=F32) + ba_ref[...])
    ig = _sigmoid(jnp.dot(yb, wx_ref[...], preferred_element_type=F32) + bx_ref[...])
    log_a = -LRU_C * r * _softplus(-lam_ref[...])
    a = jnp.exp(log_a)
    u = jnp.sqrt((1.0 - a) * (1.0 + a)) * (ig * y)
    return a, u


def _lru_prompt_kernel(xz_ref, cw_ref, cb_ref, wa_ref, ba_ref, wx_ref, bx_ref, lam_ref,
                       o_ref, hlast_ref, tail_ref, tail_scr, h_scr, a_scr, u_scr, hs_scr,
                       *, tt, conv_w, batch):
    t = pl.program_id(0)

    @pl.when(t == 0)
    def _():
        tail_scr[...] = jnp.zeros(tail_scr.shape, F32)
        h_scr[...] = jnp.zeros(h_scr.shape, F32)

    width = xz_ref.shape[2] // 2
    cw = cw_ref[...]
    for b in range(batch):
        x = xz_ref[b, :, 0:width].astype(F32)
        xe = jnp.concatenate([tail_scr[b], x], axis=0)
        y = cb_ref[...] + cw[conv_w - 1:conv_w] * x
        for j in range(1, conv_w):
            y = y + cw[conv_w - 1 - j:conv_w - j] * pltpu.roll(xe, j, 0)[SUBLANES:]
        tail_scr[b] = x[tt - SUBLANES:]
        tail_ref[b] = x[tt - SUBLANES:]
        a, u = _lru_gates(y, wa_ref, ba_ref, wx_ref, bx_ref, lam_ref)
        a_scr[b] = a
        u_scr[b] = u

    def group(gi, hs):
        base = pl.multiple_of(gi * SUBLANES, SUBLANES)
        hs = list(hs)
        for r in range(SUBLANES):
            for b in range(batch):
                hs[b] = a_scr[b, pl.ds(base + r, 1), :] * hs[b] + u_scr[b, pl.ds(base + r, 1), :]
                hs_scr[b, pl.ds(base + r, 1), :] = hs[b]
        return tuple(hs)

    hs = lax.fori_loop(0, tt // SUBLANES, group, tuple(h_scr[b] for b in range(batch)))
    for b in range(batch):
        h_scr[b] = hs[b]
        hlast_ref[b] = hs[b]
        o_ref[b] = (hs_scr[b] * _silu(xz_ref[b, :, width:2 * width].astype(F32))).astype(o_ref.dtype)


def _lru_prompt(u, p, batch, seq, xz_col):
    width = p["cw"].shape[1]
    conv_w = p["cw"].shape[0]
    tt = min(seq, LRU_TT)
    nt = seq // tt
    kern = functools.partial(_lru_prompt_kernel, tt=tt, conv_w=conv_w, batch=batch)
    const = lambda a: pl.BlockSpec(a.shape, lambda t: (0,) * a.ndim)
    ws = [p["cw"], p["cb"], p["wa"], p["ba"], p["wx"], p["bx"], p["lam"]]
    out, h_last, tail = pl.pallas_call(
        kern,
        grid=(nt,),
        in_specs=[pl.BlockSpec((batch, tt, 2 * width), lambda t: (0, t, xz_col))]
                 + [const(a) for a in ws],
        out_specs=[pl.BlockSpec((batch, tt, width), lambda t: (0, t, 0)),
                   pl.BlockSpec((batch, 1, width), lambda t: (0, 0, 0)),
                   pl.BlockSpec((batch, SUBLANES, width), lambda t: (0, 0, 0))],
        out_shape=[jax.ShapeDtypeStruct((batch, seq, width), BF16),
                   jax.ShapeDtypeStruct((batch, 1, width), F32),
                   jax.ShapeDtypeStruct((batch, SUBLANES, width), F32)],
        scratch_shapes=[pltpu.VMEM((batch, SUBLANES, width), F32), pltpu.VMEM((batch, 1, width), F32),
                        pltpu.VMEM((batch, tt, width), F32), pltpu.VMEM((batch, tt, width), F32),
                        pltpu.VMEM((batch, tt, width), F32)],
        compiler_params=_cparams("arbitrary"),
        name="lru_prompt",
    )(u.reshape(batch, seq, u.shape[1]), *ws)
    return out.reshape(batch * seq, width), h_last, tail


def _head_norm_gate(hcat, o_gate, z_gate, g_tiled, heads, hd):
    x = _sigmoid(o_gate) * hcat
    parts = []
    for h in range(heads):
        xh = x[:, h * hd:(h + 1) * hd]
        ms = jnp.mean(xh * xh, axis=-1, keepdims=True)
        parts.append(xh * lax.rsqrt(ms + EPS))
    return jnp.concatenate(parts, axis=1) * g_tiled * _silu(z_gate)


def _mlstm_chunk(cg, grow_raw, gb_rep, gbc, gn, caug_ref, m_ref, b, spread, tri_lower, tri_upper, causal,
                 heads, hd, chunk):
    width = heads * hd
    gpre = jnp.dot(cg[:, 5 * width:5 * width + LANES].astype(BF16), spread,
                   preferred_element_type=F32) + gb_rep
    lf_rep = _log_sigmoid(gpre[:, width:2 * width])
    b_rep = sum(jnp.dot(tri_lower, part, preferred_element_type=F32) for part in _split3(lf_rep))
    grow = grow_raw.astype(F32) + gbc
    lf_row = _log_sigmoid(grow)
    b_row = sum(jnp.dot(part, tri_upper, preferred_element_type=F32) for part in _split3(lf_row))

    q_all = cg[:, 0:width].astype(BF16)
    k_all = cg[:, width:2 * width].astype(F32) * (hd ** -0.5)
    v_all = cg[:, 2 * width:3 * width].astype(BF16)
    ones = jnp.ones((chunk, hd), BF16)
    h_parts = []
    for h in range(heads):
        sl = slice(h * hd, (h + 1) * hd)
        bc = b_rep[:, sl]
        ig = gpre[:, sl]
        r_row = grow[h:h + 1, :] - b_row[heads + h:heads + h + 1, :]
        m_prev = m_ref[b, h:h + 1, :]
        dlog = jnp.where(causal, bc + r_row, NEG_INF)
        g = bc + m_prev
        m_t = jnp.maximum(g, jnp.broadcast_to(jnp.max(dlog, axis=1, keepdims=True), (chunk, hd)))
        qh = q_all[:, sl]
        kf = k_all[:, sl]
        v_aug = jnp.concatenate([v_all[:, sl], ones], axis=1)
        qk = lax.dot_general(qh, kf.astype(BF16), (((1,), (1,)), ((), ())),
                             preferred_element_type=F32)
        w = jnp.exp(dlog - m_t) * qk
        inter = jnp.exp(g - m_t)
        c_old = caug_ref[b, h]
        tot = (jnp.dot(w.astype(BF16), v_aug, preferred_element_type=F32)
               + jnp.concatenate([inter, inter], axis=1)
               * jnp.dot(qh, c_old.astype(BF16), preferred_element_type=F32))
        num = tot[:, 0:hd]
        den = tot[:, hd:2 * hd]
        h_parts.append(num / jnp.maximum(jnp.abs(den), jnp.exp(-m_t)))
        m_new = m_t[chunk - 1:chunk, :]
        b_last = bc[chunk - 1:chunk, :]
        wt = jnp.exp(b_last - bc + ig - m_new)
        decay = jnp.exp(b_last + m_prev - m_new)
        kw = kf * wt
        caug_ref[b, h] = (jnp.concatenate([decay, decay], axis=1) * c_old
                          + jnp.dot(kw.T.astype(BF16), v_aug, preferred_element_type=F32))
        m_ref[b, h:h + 1, :] = m_new
    return _head_norm_gate(jnp.concatenate(h_parts, axis=1), cg[:, 3 * width:4 * width].astype(F32),
                           cg[:, 4 * width:5 * width].astype(F32), gn, heads, hd)


def _mlstm_prompt_kernel(cg_ref, gr_ref, gbrep_ref, gbc_ref, gn_ref, out_ref, caug_ref, m_ref,
                         *, heads, hd, chunk, batch):
    @pl.when(pl.program_id(0) == 0)
    def _():
        caug_ref[...] = jnp.zeros(caug_ref.shape, F32)
        m_ref[...] = jnp.zeros(m_ref.shape, F32)

    rows = lax.broadcasted_iota(jnp.int32, (chunk, chunk), 0)
    cols = lax.broadcasted_iota(jnp.int32, (chunk, chunk), 1)
    causal = rows >= cols
    tri_lower = jnp.where(causal, 1.0, 0.0).astype(BF16)
    tri_upper = jnp.where(rows <= cols, 1.0, 0.0).astype(BF16)
    src = lax.broadcasted_iota(jnp.int32, (LANES, 2 * heads * hd), 0)
    dst = lax.broadcasted_iota(jnp.int32, (LANES, 2 * heads * hd), 1)
    spread = jnp.where(dst // hd == src, 1.0, 0.0).astype(BF16)
    for b in range(batch):
        out_ref[b] = _mlstm_chunk(cg_ref[b], gr_ref[b], gbrep_ref[...], gbc_ref[...], gn_ref[...],
                                  caug_ref, m_ref, b, spread, tri_lower, tri_upper, causal,
                                  heads, hd, chunk).astype(out_ref.dtype)


def _mlstm_prompt(u, gb_rep, gb_col, gn, batch, seq, heads, hd, chunk, cg_col, gate_off):
    assert chunk == hd == LANES, "replicated per-time scalars assume chunk == head dim == lane count"
    width = heads * hd
    nc = seq // chunk
    u3 = u.reshape(batch, seq, u.shape[1])
    ugate_t = jnp.swapaxes(u3[:, :, gate_off:gate_off + BF16_ROWS], 1, 2)
    kern = functools.partial(_mlstm_prompt_kernel, heads=heads, hd=hd, chunk=chunk, batch=batch)
    const = lambda a: pl.BlockSpec(a.shape, lambda t: (0,) * a.ndim)
    out, c_aug, m_rep = pl.pallas_call(
        kern,
        grid=(nc,),
        in_specs=[pl.BlockSpec((batch, chunk, 6 * width), lambda t: (0, t, cg_col)),
                  pl.BlockSpec((batch, BF16_ROWS, chunk), lambda t: (0, 0, t)),
                  const(gb_rep), const(gb_col), const(gn)],
        out_specs=[pl.BlockSpec((batch, chunk, width), lambda t: (0, t, 0)),
                   pl.BlockSpec((batch, heads, hd, 2 * hd), lambda t: (0, 0, 0, 0)),
                   pl.BlockSpec((batch, SUBLANES, LANES), lambda t: (0, 0, 0))],
        out_shape=[jax.ShapeDtypeStruct((batch, seq, width), BF16),
                   jax.ShapeDtypeStruct((batch, heads, hd, 2 * hd), F32),
                   jax.ShapeDtypeStruct((batch, SUBLANES, LANES), F32)],
        compiler_params=_cparams("arbitrary"),
        name="mlstm_prompt",
    )(u3, ugate_t, gb_rep, gb_col, gn)
    return (out.reshape(batch * seq, width), c_aug[..., :hd], c_aug[..., hd], m_rep[:, :heads, 0])


def _lru_sample_kernel(xz_ref, conv_ref, h0_ref, cw_ref, cb_ref, wa_ref, ba_ref, wx_ref,
                       bx_ref, lam_ref, o_ref, h_ref, convn_ref, *, conv_w):
    width = xz_ref.shape[1] // 2
    x = xz_ref[:, 0:width].astype(F32)
    cw = cw_ref[...]
    y = cb_ref[...] + cw[conv_w - 1:conv_w] * x
    for j in range(conv_w - 1):
        y = y + cw[j:j + 1] * conv_ref[j]
    for j in range(conv_w - 2):
        convn_ref[j] = conv_ref[j + 1]
    convn_ref[conv_w - 2] = x
    a, u = _lru_gates(y, wa_ref, ba_ref, wx_ref, bx_ref, lam_ref)
    h = a * h0_ref[...] + u
    h_ref[...] = h
    o_ref[...] = (h * _silu(xz_ref[:, width:2 * width].astype(F32))).astype(o_ref.dtype)


def _lru_sample(u, conv_t, h0, p, xz_col):
    m = u.shape[0]
    width = p["cw"].shape[1]
    conv_w = p["cw"].shape[0]
    kern = functools.partial(_lru_sample_kernel, conv_w=conv_w)
    row = lambda c: pl.BlockSpec((m, width), lambda i, c=c: (0, c))
    full = lambda a: pl.BlockSpec(a.shape, lambda i: (0,) * a.ndim)
    ws = [p["cw"], p["cb"], p["wa"], p["ba"], p["wx"], p["bx"], p["lam"]]
    return pl.pallas_call(
        kern,
        grid=(1,),
        in_specs=[pl.BlockSpec((m, 2 * width), lambda i: (0, xz_col)), full(conv_t), full(h0)]
                 + [full(a) for a in ws],
        out_specs=[row(0), row(0), full(conv_t)],
        out_shape=[jax.ShapeDtypeStruct((m, width), BF16),
                   jax.ShapeDtypeStruct((m, width), F32),
                   jax.ShapeDtypeStruct(conv_t.shape, F32)],
        compiler_params=_cparams("arbitrary"),
        name="lru_sample",
    )(u, conv_t, h0, *ws)


def _mlstm_sample_kernel(cg_ref, gbr_ref, gn_ref,
                         c_ref, n_ref, m_ref, out_ref, cn_ref, nn_ref, mn_ref, h_scr,
                         *, heads, hd, sb):
    width = heads * hd
    gcol = cg_ref[:, 5 * width:5 * width + LANES] + gbr_ref[...]
    ig = gcol[:, 0:heads]
    lf = _log_sigmoid(gcol[:, heads:2 * heads])
    m_prev = m_ref[...]
    g = lf + m_prev
    m_t = jnp.maximum(g, ig)
    w_in = jnp.exp(ig - m_t)
    inter = jnp.exp(g - m_t)
    floor = jnp.exp(-m_t)
    mn_ref[...] = m_t
    q_all = cg_ref[:, 0:width]
    k_all = cg_ref[:, width:2 * width] * (hd ** -0.5)
    v_all = cg_ref[:, 2 * width:3 * width]
    for h in range(heads):
        sl = slice(h * hd, (h + 1) * hd)
        qk = jnp.sum(q_all[:, sl] * k_all[:, sl], axis=1, keepdims=True)
        for s in range(sb):
            q_row = q_all[s:s + 1, sl]
            k_row = k_all[s:s + 1, sl]
            v_row = v_all[s:s + 1, sl]
            q_col = jnp.broadcast_to(q_row, (hd, hd)).T
            k_col = jnp.broadcast_to(k_row, (hd, hd)).T
            c_old = c_ref[s, h]
            n_old = n_ref[s, h:h + 1, :]
            w_s = w_in[s:s + 1, h:h + 1]
            i_s = inter[s:s + 1, h:h + 1]
            num = w_s * qk[s:s + 1] * v_row + i_s * jnp.sum(q_col * c_old, axis=0, keepdims=True)
            den = w_s * qk[s:s + 1] + i_s * jnp.sum(q_row * n_old, axis=1, keepdims=True)
            h_scr[s:s + 1, sl] = num / jnp.maximum(jnp.abs(den), floor[s:s + 1, h:h + 1])
            cn_ref[s, h] = i_s * c_old + (w_s * k_col) * v_row
            nn_ref[s, h:h + 1, :] = i_s * n_old + w_s * k_row
    out_ref[...] = _head_norm_gate(h_scr[...], cg_ref[:, 3 * width:4 * width],
                                   cg_ref[:, 4 * width:5 * width], gn_ref[...],
                                   heads, hd).astype(out_ref.dtype)


def _mlstm_sample(u, gb_row, gn, c0, n0, m0, heads, hd, cg_col):
    m = u.shape[0]
    width = heads * hd
    sb = SUBLANES
    kern = functools.partial(_mlstm_sample_kernel, heads=heads, hd=hd, sb=sb)
    row = lambda c: pl.BlockSpec((sb, width), lambda i, c=c: (i, c))
    const = lambda a: pl.BlockSpec(a.shape, lambda i: (0,) * a.ndim)
    c_spec = pl.BlockSpec((sb, heads, hd, hd), lambda i: (i, 0, 0, 0))
    n_spec = pl.BlockSpec((sb, heads, hd), lambda i: (i, 0, 0))
    m_spec = pl.BlockSpec((sb, heads), lambda i: (i, 0))
    return pl.pallas_call(
        kern,
        grid=(m // sb,),
        in_specs=[pl.BlockSpec((sb, 6 * width), lambda i: (i, cg_col)), const(gb_row), const(gn),
                  c_spec, n_spec, m_spec],
        out_specs=[row(0), c_spec, n_spec, m_spec],
        out_shape=[jax.ShapeDtypeStruct((m, width), BF16),
                   jax.ShapeDtypeStruct(c0.shape, F32),
                   jax.ShapeDtypeStruct(n0.shape, F32),
                   jax.ShapeDtypeStruct(m0.shape, F32)],
        scratch_shapes=[pltpu.VMEM((sb, width), F32)],
        compiler_params=_cparams("parallel"),
        name="mlstm_sample",
    )(u, gb_row, gn, c0, n0, m0)


def _merge_kernel(x_ref, a_ref, b_ref, c_ref, mg_ref, mb_ref, wb_ref, wo_ref, y_ref, *, d):
    merged = None
    for j, br in enumerate((a_ref, b_ref, c_ref)):
        proj = jnp.dot(br[...], wb_ref[j], preferred_element_type=F32)
        gate = _sigmoid(mg_ref[:, j * d:(j + 1) * d].astype(F32) + mb_ref[:, j * d:(j + 1) * d])
        merged = gate * proj if merged is None else merged + gate * proj
    y_ref[...] = x_ref[...] + jnp.dot(merged.astype(BF16), wo_ref[...],
                                      preferred_element_type=F32)


def _merge(x, a, b, c, u, mg_col, mb, wb, wo):
    m, d = x.shape
    width = a.shape[1]
    tm = min(m, MERGE_TM)
    kern = functools.partial(_merge_kernel, d=d)
    row = lambda w: pl.BlockSpec((tm, w), lambda i: (i, 0))
    const = lambda arr: pl.BlockSpec(arr.shape, lambda i: (0,) * arr.ndim)
    return pl.pallas_call(
        kern,
        grid=(m // tm,),
        in_specs=[row(d), row(width), row(width), row(width),
                  pl.BlockSpec((tm, mb.shape[1]), lambda i: (i, mg_col)),
                  const(mb), const(wb), const(wo)],
        out_specs=row(d),
        out_shape=jax.ShapeDtypeStruct((m, d), F32),
        compiler_params=_cparams("parallel"),
        name="merge",
    )(x, a, b, c, u, mb, wb, wo)


def _expand_block_diag(w):
    nb, c, _ = w.shape
    eye = jnp.eye(nb, dtype=w.dtype)
    return (eye[:, None, :, None] * w[:, :, None, :]).reshape(nb * c, nb * c)


def kernel(x_prompt, x_sample, cache_k, cache_v, page_table, state_conv, state_lru_h, state_mlstm_C, state_mlstm_n, state_mlstm_m, norm_g, w_in, a_qnorm_g, a_knorm_g, a_lambda, a_subln_g, b_conv_w, b_conv_b, b_wa, b_ba, b_wx, b_bx, b_lambda, c_gate_b, c_outnorm_g, merge_b, w_branch, w_out):
    batch, seq, d = x_prompt.shape
    db = x_sample.shape[0]
    depth = w_in.shape[0]
    heads = cache_k.shape[3]
    v_dim = cache_v.shape[4]
    qk_dim = a_qnorm_g.shape[1]
    a_width = heads * v_dim
    b_width = b_conv_w.shape[2]
    c_heads = state_mlstm_C.shape[2]
    c_hd = state_mlstm_C.shape[3]
    c_width = c_heads * c_hd
    n_branch = w_branch.shape[1]
    chunk = min(seq, MLSTM_CHUNK)
    assert a_width == b_width == c_width, "column-block addressing assumes equal branch widths"
    width = a_width
    n_gate = 2 * c_heads
    n_merge = n_branch * d
    off_a, off_az, off_b, off_c = 0, 3 * width, 4 * width, 6 * width
    off_g, off_mg = off_c + 5 * width, off_c + 6 * width
    n_packed = off_mg + n_merge
    assert off_mg % n_merge == 0 and off_c % (6 * width) == 0 and width % LANES == 0 and n_gate <= BF16_ROWS
    a_col, az_col, xz_col, cg_col, mg_col = 0, off_az // v_dim, off_b // (2 * width), off_c // (6 * width), off_mg // n_merge
    tn = n_packed // INPROJ_COL_STEPS
    assert n_packed % INPROJ_COL_STEPS == 0 and tn % LANES == 0

    n_phys, page = cache_k.shape[1], cache_k.shape[2]
    ck2 = cache_k.reshape(depth, n_phys, page * heads, 2 * qk_dim)
    cv2 = cache_v.reshape(depth, n_phys, page * heads, v_dim)
    bd = _block_diag_ones(qk_dim)

    xp = x_prompt.reshape(batch * seq, d)
    xs = x_sample.reshape(db, d)
    outs = {name: [] for name in ("kp", "vp", "ks", "vs", "cp", "cs", "hp", "hs",
                                  "Cp", "Cs", "np", "ns", "mp", "ms")}
    conv_w = b_conv_w.shape[1]

    for l in range(depth):
        lam_init = 0.8 - 0.6 * math.exp(-0.3 * l)
        wl = w_in[l]
        w_packed = jnp.concatenate([wl[:, :off_g + n_gate], jnp.zeros((d, width - n_gate), wl.dtype),
                                    wl[:, off_g + n_gate:]], axis=1).astype(BF16)
        g_norm = norm_g[l][None, :]
        gq = jnp.tile(a_qnorm_g[l], 2 * heads)[None, :]
        gk = jnp.tile(a_knorm_g[l], 2 * heads)[None, :]
        g_sub = a_subln_g[l][None, :]
        lru = dict(cw=b_conv_w[l], cb=b_conv_b[l][None, :],
                   wa=_expand_block_diag(b_wa[l]).astype(BF16), ba=b_ba[l][None, :],
                   wx=_expand_block_diag(b_wx[l]).astype(BF16), bx=b_bx[l][None, :],
                   lam=b_lambda[l][None, :])
        gb_row = jnp.pad(c_gate_b[l], (0, LANES - n_gate))[None, :]
        gb_col = jnp.broadcast_to(jnp.pad(c_gate_b[l], (0, BF16_ROWS - n_gate))[:, None],
                                  (BF16_ROWS, chunk))
        gb_rep = jnp.repeat(c_gate_b[l], c_hd)[None, :]
        gn = jnp.tile(c_outnorm_g[l], c_heads)[None, :]
        mb = merge_b[l][None, :]
        wb = w_branch[l].astype(BF16)
        wo = w_out[l].astype(BF16)

        u = _inproj(xp, g_norm, w_packed, tn, BF16)
        q_t, k_t, v_t, k_out, v_out = _qkv_prompt(u, gq, gk, bd, batch, seq, heads, qk_dim, v_dim, a_col)
        a_out = _attn_prompt(q_t, k_t, v_t, u, a_lambda[l], g_sub, az_col, lam_init)
        b_out, h_last, tail = _lru_prompt(u, lru, batch, seq, xz_col)
        c_out, c_new, n_new, m_new = _mlstm_prompt(
            u, gb_rep, gb_col, gn, batch, seq, c_heads, c_hd, chunk, cg_col, off_g)
        xp = _merge(xp, a_out, b_out, c_out, u, mg_col, mb, wb, wo)
        outs["kp"].append(k_out.reshape(batch, seq, heads, 2 * qk_dim))
        outs["vp"].append(v_out.reshape(batch, seq, heads, v_dim))
        outs["cp"].append(tail[:, SUBLANES - (conv_w - 1):, :])
        outs["hp"].append(h_last[:, 0, :])
        outs["Cp"].append(c_new)
        outs["np"].append(n_new)
        outs["mp"].append(m_new)

        us = _inproj(xs, g_norm, w_packed, tn, F32)
        qn_s, k_s, v_s = _qkv_sample(us, gq, gk, bd, qk_dim, width, a_col)
        a_s = _attn_sample(l, page_table, ck2, cv2, qn_s, k_s, v_s, us[:, off_az:off_az + width],
                           a_lambda[l], g_sub, lam_init)
        b_s, h_s, conv_s = _lru_sample(us, jnp.swapaxes(state_conv[l], 0, 1), state_lru_h[l], lru, xz_col)
        c_s, cs_new, ns_new, ms_new = _mlstm_sample(
            us, gb_row, gn, state_mlstm_C[l], state_mlstm_n[l], state_mlstm_m[l], c_heads, c_hd, cg_col)
        xs = _merge(xs, a_s, b_s, c_s, us, mg_col, mb, wb, wo)
        outs["ks"].append(k_s.reshape(db, 1, heads, 2 * qk_dim))
        outs["vs"].append(v_s.reshape(db, 1, heads, v_dim))
        outs["cs"].append(jnp.swapaxes(conv_s, 0, 1))
        outs["hs"].append(h_s)
        outs["Cs"].append(cs_new)
        outs["ns"].append(ns_new)
        outs["ms"].append(ms_new)

    st = {k: jnp.stack(v) for k, v in outs.items()}
    return (xp.reshape(batch, seq, d), xs.reshape(db, 1, d),
            st["kp"], st["vp"], st["ks"], st["vs"], st["cp"], st["cs"], st["hp"], st["hs"],
            st["Cp"], st["Cs"], st["np"], st["ns"], st["mp"], st["ms"])
```
